```python
import jax
import jax.numpy as jnp
from jax import lax
import numpy as np

D_MODEL = 1024
BATCH = 2
SEQ = 8192
DEPTH = 4

EPS = 1e-6
N_EVEN = (DEPTH + 1) // 2
N_ODD = DEPTH // 2

RET_HEADS = 4
RET_DK = D_MODEL // 8
RET_DV = D_MODEL // 4
RET_CHUNK = 128
ROPE_BASE = 10000.0

SSD_D_INNER = D_MODEL
SSD_HEADDIM = 64
SSD_HEADS = SSD_D_INNER // SSD_HEADDIM
SSD_GROUPS = 2
SSD_STATE = 128
SSD_CONV = 4
SSD_CHUNK = 128
SSD_CONV_DIM = SSD_D_INNER + 2 * SSD_GROUPS * SSD_STATE

LRU_WIDTH = D_MODEL
LRU_BLOCKS = 8
LRU_BLOCK = LRU_WIDTH // LRU_BLOCKS
LRU_C = 8.0
LRU_CONV = 4

SB_HEADS = 8
SB_HEAD_DIM = 64
SB_BLOCK = 128

D_FF = 4 * D_MODEL

EVEN_SPLITS = (RET_HEADS * RET_DK, RET_HEADS * RET_DK, RET_HEADS * RET_DV, RET_HEADS * RET_DV, SSD_D_INNER, SSD_CONV_DIM, SSD_HEADS)
EVEN_IN = sum(EVEN_SPLITS)
EVEN_OUT = RET_HEADS * RET_DV + SSD_D_INNER
ODD_SPLITS = (LRU_WIDTH, LRU_WIDTH, SB_HEADS * SB_HEAD_DIM, SB_HEADS * SB_HEAD_DIM, SB_HEADS * SB_HEAD_DIM)
ODD_IN = sum(ODD_SPLITS)
ODD_OUT = LRU_WIDTH + SB_HEADS * SB_HEAD_DIM

kernel_name = 'hybrid_retention_ssd_rglru_stickbreak'


def _split(a, sizes):
    return jnp.split(a, [int(s) for s in np.cumsum(sizes)[:-1]], axis=-1)


def rmsnorm(x, g):
    xf = x.astype(jnp.float32)
    y = xf * lax.rsqrt(jnp.mean(xf * xf, axis=-1, keepdims=True) + EPS)
    return (y * g.astype(jnp.float32)).astype(x.dtype)


def head_groupnorm(y, g):
    B, T, H, dv = y.shape
    mu = jnp.mean(y, axis=-1, keepdims=True)
    yc = y - mu
    yn = yc * lax.rsqrt(jnp.mean(yc * yc, axis=-1, keepdims=True) + EPS)
    return (yn * g.astype(jnp.float32).reshape(H, dv)).reshape(B, T, H * dv)


def causal_dwconv(x, w, b):
    K, C = w.shape
    y = lax.conv_general_dilated(x, w.astype(x.dtype)[:, None, :], window_strides=(1,), padding=[(K - 1, 0)], dimension_numbers=('NWC', 'WIO', 'NWC'), feature_group_count=C)
    return y + b.astype(x.dtype)


def rotary(x, pos):
    half = x.shape[-1] // 2
    inv = ROPE_BASE ** (-jnp.arange(half, dtype=jnp.float32) / half)
    ang = pos.astype(jnp.float32)[:, None] * inv[None, :]
    cos = jnp.cos(ang)[None, :, None, :]
    sin = jnp.sin(ang)[None, :, None, :]
    x1, x2 = x[..., :half], x[..., half:]
    return jnp.concatenate([x1 * cos - x2 * sin, x1 * sin + x2 * cos], axis=-1)


def retention_chunkwise(q, k, v, log_gamma):
    B, T, H, dk = q.shape
    dv = v.shape[-1]
    C = RET_CHUNK
    n = T // C
    q = q.reshape(B, n, C, H, dk)
    k = k.reshape(B, n, C, H, dk)
    v = v.reshape(B, n, C, H, dv)
    idx = jnp.arange(C, dtype=jnp.float32)
    rel = idx[:, None] - idx[None, :]
    causal = rel >= 0
    decay = jnp.where(causal[None], jnp.exp(log_gamma[:, None, None] * jnp.maximum(rel, 0.0)[None]), 0.0)
    scores = jnp.einsum('bcihd,bcjhd->bchij', q, k) * decay[None, None]
    inner = jnp.einsum('bchij,bcjhe->bcihe', scores, v)
    k_dec = jnp.exp(log_gamma[None, :] * (C - 1 - idx)[:, None])
    chunk_kv = jnp.einsum('bcjhd,jh,bcjhe->bchde', k, k_dec, v)
    chunk_decay = jnp.exp(log_gamma * C)

    def step(S, kv):
        return S * chunk_decay[None, :, None, None] + kv, S

    _, S_prev = lax.scan(step, jnp.zeros((B, H, dk, dv), jnp.float32), jnp.moveaxis(chunk_kv, 1, 0))
    S_prev = jnp.moveaxis(S_prev, 0, 1)
    q_dec = jnp.exp(log_gamma[None, :] * (idx + 1.0)[:, None])
    cross = jnp.einsum('bcihd,ih,bchde->bcihe', q, q_dec, S_prev)
    return (inner + cross).reshape(B, T, H, dv)


def ssd_chunked(x, dt, A, Bm, Cm):
    Bsz, T, H, P = x.shape
    G, N = Bm.shape[2], Bm.shape[3]
    R = H // G
    Q = SSD_CHUNK
    n = T // Q
    xr = (x * dt[..., None]).reshape(Bsz, n, Q, G, R, P)
    dA = (dt * A).reshape(Bsz, n, Q, G, R)
    Acum = jnp.cumsum(dA, axis=2)
    Br = Bm.reshape(Bsz, n, Q, G, N)
    Cr = Cm.reshape(Bsz, n, Q, G, N)
    seg = Acum[:, :, :, None] - Acum[:, :, None, :]
    causal = jnp.tril(jnp.ones((Q, Q), dtype=bool))[None, None, :, :, None, None]
    L = jnp.exp(jnp.where(causal, seg, -jnp.inf))
    CB = jnp.einsum('bcigs,bcjgs->bcijg', Cr, Br)
    y_diag = jnp.einsum('bcijg,bcijgr,bcjgrp->bcigrp', CB, L, xr)
    decay_states = jnp.exp(Acum[:, :, -1:] - Acum)
    states = jnp.einsum('bcjgs,bcjgr,bcjgrp->bcgrps', Br, decay_states, xr)
    chunk_decay = jnp.exp(Acum[:, :, -1])

    def step(S, inp):
        st, dec = inp
        return S * dec[..., None, None] + st, S

    _, S_prev = lax.scan(step, jnp.zeros((Bsz, G, R, P, N), jnp.float32), (jnp.moveaxis(states, 1, 0), jnp.moveaxis(chunk_decay, 1, 0)))
    S_prev = jnp.moveaxis(S_prev, 0, 1)
    y_off = jnp.einsum('bcigs,bcgrps,bcigr->bcigrp', Cr, S_prev, jnp.exp(Acum))
    return (y_diag + y_off).reshape(Bsz, T, H, P)


def rg_lru(x, wa, ba, wx, bx, lam):
    B, T, W = x.shape
    xb = x.reshape(B, T, LRU_BLOCKS, LRU_BLOCK)
    r = jax.nn.sigmoid(jnp.einsum('btki,kij->btkj', xb, wa.astype(x.dtype)) + ba.astype(x.dtype)).reshape(B, T, W)
    i = jax.nn.sigmoid(jnp.einsum('btki,kij->btkj', xb, wx.astype(x.dtype)) + bx.astype(x.dtype)).reshape(B, T, W)
    log_a = -LRU_C * r * jax.nn.softplus(-lam.astype(x.dtype))
    a = jnp.exp(log_a)
    u = jnp.sqrt(-jnp.expm1(2.0 * log_a)) * (i * x)

    def combine(c1, c2):
        a1, b1 = c1
        a2, b2 = c2
        return a1 * a2, a2 * b1 + b2

    _, h = lax.associative_scan(combine, (a, u), axis=1)
    return h


def stick_breaking(q, k, v):
    B, T, H, d = q.shape
    nb = T // SB_BLOCK
    scale = d ** -0.5
    qb = q.reshape(B, nb, SB_BLOCK, H, d).transpose(1, 0, 3, 2, 4)
    kt = k.transpose(0, 2, 1, 3)
    vt = v.transpose(0, 2, 1, 3)
    kpos = jnp.arange(T)

    def block(args):
        qblk, bi = args
        qpos = bi * SB_BLOCK + jnp.arange(SB_BLOCK)
        z = jnp.einsum('bhqd,bhkd->bhqk', qblk, kt) * scale
        mask = (kpos[None, :] < qpos[:, None])[None, None]
        log_beta = jax.nn.log_sigmoid(z)
        log_1mb = jnp.where(mask, jax.nn.log_sigmoid(-z), 0.0)
        suffix = jnp.flip(jnp.cumsum(jnp.flip(log_1mb, -1), axis=-1), -1) - log_1mb
        w = jnp.where(mask, jnp.exp(log_beta + suffix), 0.0)
        return jnp.einsum('bhqk,bhkd->bhqd', w, vt)

    out = lax.map(block, (qb, jnp.arange(nb)))
    return out.transpose(1, 0, 3, 2, 4).reshape(B, T, H * d)


def even_mixer(h, w_in, w_out, qn, kn, gn, conv_w, conv_b, dt_bias, a_log, d_skip, ssd_g):
    B, T, _ = h.shape
    f32 = jnp.float32
    q, k, v, g, z, xbc, dt_raw = _split(h @ w_in, EVEN_SPLITS)
    pos = jnp.arange(T)
    q = rotary(rmsnorm(q.reshape(B, T, RET_HEADS, RET_DK).astype(f32), qn), pos)
    k = rotary(rmsnorm(k.reshape(B, T, RET_HEADS, RET_DK).astype(f32), kn), pos) * (RET_DK ** -0.5)
    v = v.reshape(B, T, RET_HEADS, RET_DV).astype(f32)
    log_gamma = jnp.log1p(-jnp.exp2(-5.0 - jnp.arange(RET_HEADS, dtype=f32)))
    ya = retention_chunkwise(q, k, v, log_gamma)
    ya = jax.nn.silu(g.astype(f32)) * head_groupnorm(ya, gn)
    xbc = jax.nn.silu(causal_dwconv(xbc, conv_w, conv_b).astype(f32))
    xs, bm, cm = _split(xbc, (SSD_D_INNER, SSD_GROUPS * SSD_STATE, SSD_GROUPS * SSD_STATE))
    dt = jax.nn.softplus(dt_raw.astype(f32) + dt_bias.astype(f32))
    A = -jnp.exp(a_log.astype(f32))
    xs = xs.reshape(B, T, SSD_HEADS, SSD_HEADDIM)
    yb = ssd_chunked(xs, dt, A, bm.reshape(B, T, SSD_GROUPS, SSD_STATE), cm.reshape(B, T, SSD_GROUPS, SSD_STATE))
    yb = yb + xs * d_skip.astype(f32)[:, None]
    yb = yb.reshape(B, T, SSD_D_INNER) * jax.nn.silu(z.astype(f32))
    yb = rmsnorm(yb.reshape(B, T, SSD_GROUPS, SSD_D_INNER // SSD_GROUPS), ssd_g.reshape(SSD_GROUPS, SSD_D_INNER // SSD_GROUPS)).reshape(B, T, SSD_D_INNER)
    y = jnp.concatenate([ya, yb], axis=-1).astype(h.dtype)
    return y @ w_out


def odd_mixer(h, w_in, w_out, conv_w, conv_b, wa, ba, wx, bx, lam, qn, kn):
    B, T, _ = h.shape
    f32 = jnp.float32
    gate, xc, q, k, v = _split(h @ w_in, ODD_SPLITS)
    xc = causal_dwconv(xc, conv_w, conv_b).astype(f32)
    yc = rg_lru(xc, wa, ba, wx, bx, lam) * jax.nn.gelu(gate.astype(f32))
    shp = (B, T, SB_HEADS, SB_HEAD_DIM)
    q = rmsnorm(q.reshape(shp).astype(f32), qn)
    k = rmsnorm(k.reshape(shp).astype(f32), kn)
    yd = stick_breaking(q, k, v.reshape(shp).astype(f32))
    y = jnp.concatenate([yc, yd], axis=-1).astype(h.dtype)
    return y @ w_out


def setup_inputs(seed: int = 0) -> dict:
    key = jax.random.key(seed)
    keys = iter(jax.random.split(key, 40))
    f32 = jnp.float32

    def nrm(shape, scale):
        return jax.random.normal(next(keys), shape, f32) * scale

    def gain(shape):
        return 1.0 + nrm(shape, 0.05)

    def unif(shape, lo, hi):
        return jax.random.uniform(next(keys), shape, f32, lo, hi)

    x = nrm((BATCH, SEQ, D_MODEL), 1.0)
    norm_mix = gain((DEPTH, D_MODEL))
    norm_mlp = gain((DEPTH, D_MODEL))
    mlp_w1 = nrm((DEPTH, D_MODEL, D_FF), D_MODEL ** -0.5)
    mlp_w2 = nrm((DEPTH, D_FF, D_MODEL), D_FF ** -0.5)
    ev_w_in = nrm((N_EVEN, D_MODEL, EVEN_IN), D_MODEL ** -0.5)
    ev_w_out = nrm((N_EVEN, EVEN_OUT, D_MODEL), EVEN_OUT ** -0.5)
    ret_qn = gain((N_EVEN, RET_DK))
    ret_kn = gain((N_EVEN, RET_DK))
    ret_gn = gain((N_EVEN, RET_HEADS * RET_DV))
    ssd_conv_w = nrm((N_EVEN, SSD_CONV, SSD_CONV_DIM), SSD_CONV ** -0.5)
    ssd_conv_b = nrm((N_EVEN, SSD_CONV_DIM), 0.02)
    dt0 = jnp.exp(unif((N_EVEN, SSD_HEADS), float(np.log(1e-3)), float(np.log(1e-1))))
    ssd_dt_bias = dt0 + jnp.log(-jnp.expm1(-dt0))
    ssd_a_log = jnp.log(unif((N_EVEN, SSD_HEADS), 1.0, 16.0))
    ssd_d = gain((N_EVEN, SSD_HEADS))
    ssd_norm = gain((N_EVEN, SSD_D_INNER))
    od_w_in = nrm((N_ODD, D_MODEL, ODD_IN), D_MODEL ** -0.5)
    od_w_out = nrm((N_ODD, ODD_OUT, D_MODEL), ODD_OUT ** -0.5)
    lru_conv_w = nrm((N_ODD, LRU_CONV, LRU_WIDTH), LRU_CONV ** -0.5)
    lru_conv_b = nrm((N_ODD, LRU_WIDTH), 0.02)
    lru_wa = nrm((N_ODD, LRU_BLOCKS, LRU_BLOCK, LRU_BLOCK), LRU_BLOCK ** -0.5)
    lru_ba = nrm((N_ODD, LRU_BLOCKS, LRU_BLOCK), 0.1)
    lru_wx = nrm((N_ODD, LRU_BLOCKS, LRU_BLOCK, LRU_BLOCK), LRU_BLOCK ** -0.5)
    lru_bx = nrm((N_ODD, LRU_BLOCKS, LRU_BLOCK), 0.1)
    a_c = unif((N_ODD, LRU_WIDTH), 0.81, 0.998)
    s = a_c ** (1.0 / LRU_C)
    lru_lam = jnp.log(s) - jnp.log1p(-s)
    sb_qn = gain((N_ODD, SB_HEAD_DIM))
    sb_kn = gain((N_ODD, SB_HEAD_DIM))
    return {'x': x, 'norm_mix': norm_mix, 'norm_mlp': norm_mlp, 'mlp_w1': mlp_w1, 'mlp_w2': mlp_w2,
            'ev_w_in': ev_w_in, 'ev_w_out': ev_w_out, 'ret_qn': ret_qn, 'ret_kn': ret_kn, 'ret_gn': ret_gn,
            'ssd_conv_w': ssd_conv_w, 'ssd_conv_b': ssd_conv_b, 'ssd_dt_bias': ssd_dt_bias, 'ssd_a_log': ssd_a_log,
            'ssd_d': ssd_d, 'ssd_norm': ssd_norm, 'od_w_in': od_w_in, 'od_w_out': od_w_out,
            'lru_conv_w': lru_conv_w, 'lru_conv_b': lru_conv_b, 'lru_wa': lru_wa, 'lru_ba': lru_ba,
            'lru_wx': lru_wx, 'lru_bx': lru_bx, 'lru_lam': lru_lam, 'sb_qn': sb_qn, 'sb_kn': sb_kn}


def reference(x, norm_mix, norm_mlp, mlp_w1, mlp_w2, ev_w_in, ev_w_out, ret_qn, ret_kn, ret_gn, ssd_conv_w, ssd_conv_b, ssd_dt_bias, ssd_a_log, ssd_d, ssd_norm, od_w_in, od_w_out, lru_conv_w, lru_conv_b, lru_wa, lru_ba, lru_wx, lru_bx, lru_lam, sb_qn, sb_kn):
    for l in range(DEPTH):
        h = rmsnorm(x, norm_mix[l])
        if l % 2 == 0:
            e = l // 2
            x = x + even_mixer(h, ev_w_in[e], ev_w_out[e], ret_qn[e], ret_kn[e], ret_gn[e], ssd_conv_w[e], ssd_conv_b[e], ssd_dt_bias[e], ssd_a_log[e], ssd_d[e], ssd_norm[e])
        else:
            o = l // 2
            x = x + odd_mixer(h, od_w_in[o], od_w_out[o], lru_conv_w[o], lru_conv_b[o], lru_wa[o], lru_ba[o], lru_wx[o], lru_bx[o], lru_lam[o], sb_qn[o], sb_kn[o])
        h = rmsnorm(x, norm_mlp[l])
        x = x + jnp.square(jax.nn.relu(h @ mlp_w1[l])) @ mlp_w2[l]
    return x
```

```python
import functools
import math

import numpy as np
import jax
import jax.numpy as jnp
from jax import lax
from jax.experimental import pallas as pl
from jax.experimental.pallas import tpu as pltpu

F32 = jnp.float32
BF16 = jnp.bfloat16

D_MODEL = 1024
EPS = 1e-6

RET_HEADS = 4
RET_DK = 128
RET_DV = 256
ROPE_BASE = 10000.0
RET_LOG_GAMMA = tuple(float(np.log1p(-(2.0 ** (-5.0 - h)))) for h in range(RET_HEADS))

SSD_D_INNER = 1024
SSD_HEADDIM = 64
SSD_HEADS = 16
SSD_GROUPS = 2
SSD_STATE = 128
SSD_CONV = 4
SSD_CONV_DIM = SSD_D_INNER + 2 * SSD_GROUPS * SSD_STATE
CHUNK = 128

LRU_WIDTH = 1024
LRU_BLOCKS = 8
LRU_BLOCK = 128
LRU_C = 8.0
LRU_CONV = 4

SB_HEADS = 8
SB_HEAD_DIM = 64
SB_WIDTH = SB_HEADS * SB_HEAD_DIM

D_FF = 4 * D_MODEL

EVEN_IN = 5648
EVEN_IN_PAD = 5760
EVEN_OUT = 2048
ODD_IN = 3584
ODD_OUT = 1536

EV_Q, EV_K, EV_V, EV_G, EV_Z, EV_XBC, EV_DT = 0, 512, 1024, 2048, 3072, 4096, 5632
OD_GATE, OD_XC, OD_Q, OD_K, OD_V = 0, 1024, 2048, 2560, 3072

SUBLANES = 8
CONV_PAD = 8
VMEM_LIMIT = 56 * 1024 * 1024


def _params(sem):
    return pltpu.CompilerParams(dimension_semantics=sem, vmem_limit_bytes=VMEM_LIMIT)


def _dot(a, b):
    return jnp.dot(a, b, preferred_element_type=F32)


def _dot_nt(a, b):
    return lax.dot_general(a, b, (((1,), (1,)), ((), ())), preferred_element_type=F32)


def _dot_tn(a, b):
    return lax.dot_general(a, b, (((0,), (0,)), ((), ())), preferred_element_type=F32)


def _split3(x):
    hi = x.astype(BF16)
    r = x - hi.astype(F32)
    mid = r.astype(BF16)
    lo = (r - mid.astype(F32)).astype(BF16)
    return hi, mid, lo


def _dot_exact_lhs(x, m):
    hi, mid, lo = _split3(x)
    return _dot(hi, m) + _dot(mid, m) + _dot(lo, m)


def _dot_exact_rhs(m, x):
    hi, mid, lo = _split3(x)
    return _dot(m, hi) + _dot(m, mid) + _dot(m, lo)


def _sigmoid(x):
    return 1.0 / (1.0 + jnp.exp(-x))


def _silu(x):
    return x * _sigmoid(x)


def _softplus(x):
    return jnp.maximum(x, 0.0) + jnp.log(1.0 + jnp.exp(-jnp.abs(x)))


def _gelu_tanh(x):
    c = math.sqrt(2.0 / math.pi)
    return x * (0.5 * (1.0 + jnp.tanh(c * (x + 0.044715 * (x * x * x)))))


def _rms(x, g):
    return x * lax.rsqrt(jnp.mean(x * x, axis=-1, keepdims=True) + EPS) * g


def _norm_matmul_kernel(x_ref, g_ref, w_ref, o_ref, h_scr):
    @pl.when(pl.program_id(1) == 0)
    def _():
        h_scr[...] = _rms(x_ref[...], g_ref[...]).astype(BF16)

    o_ref[...] = _dot(h_scr[...], w_ref[...]).astype(o_ref.dtype)


def _norm_matmul(x, g, w, tn, out_dtype=F32):
    n, d = x.shape
    n_out = w.shape[1]
    tm = min(1024, n)
    return pl.pallas_call(
        _norm_matmul_kernel,
        grid=(n // tm, n_out // tn),
        in_specs=[
            pl.BlockSpec((tm, d), lambda i, j: (i, 0)),
            pl.BlockSpec((1, d), lambda i, j: (0, 0)),
            pl.BlockSpec((d, tn), lambda i, j: (0, j)),
        ],
        out_specs=pl.BlockSpec((tm, tn), lambda i, j: (i, j)),
        out_shape=jax.ShapeDtypeStruct((n, n_out), out_dtype),
        scratch_shapes=[pltpu.VMEM((tm, d), BF16)],
        compiler_params=_params(("parallel", "arbitrary")),
        name="norm_matmul",
    )(x, g, w)


def _mlp_kernel(x_ref, g_ref, w1_ref, w2_ref, o_ref, h_scr, acc_scr):
    f = pl.program_id(1)

    @pl.when(f == 0)
    def _():
        h_scr[...] = _rms(x_ref[...], g_ref[...]).astype(BF16)
        acc_scr[...] = jnp.zeros_like(acc_scr)

    a = _dot(h_scr[...], w1_ref[...])
    a = jnp.square(jnp.maximum(a, 0.0)).astype(BF16)
    acc_scr[...] += _dot(a, w2_ref[...])

    @pl.when(f == pl.num_programs(1) - 1)
    def _():
        o_ref[...] = x_ref[...] + acc_scr[...]


def _mlp(x, g, w1, w2):
    n, d = x.shape
    dff = w1.shape[1]
    tm = min(1024, n)
    tf = 512
    return pl.pallas_call(
        _mlp_kernel,
        grid=(n // tm, dff // tf),
        in_specs=[
            pl.BlockSpec((tm, d), lambda i, f: (i, 0)),
            pl.BlockSpec((1, d), lambda i, f: (0, 0)),
            pl.BlockSpec((d, tf), lambda i, f: (0, f)),
            pl.BlockSpec((tf, d), lambda i, f: (f, 0)),
        ],
        out_specs=pl.BlockSpec((tm, d), lambda i, f: (i, 0)),
        out_shape=jax.ShapeDtypeStruct((n, d), F32),
        scratch_shapes=[pltpu.VMEM((tm, d), BF16), pltpu.VMEM((tm, d), F32)],
        compiler_params=_params(("parallel", "arbitrary")),
        name="mlp",
    )(x, g, w1, w2)


def _outproj1_kernel(x_ref, y_ref, w_ref, o_ref):
    o_ref[...] = x_ref[...] + _dot(y_ref[...], w_ref[...])


def _outproj2_kernel(x_ref, ya_ref, yb_ref, wa_ref, wb_ref, o_ref):
    o_ref[...] = x_ref[...] + _dot(ya_ref[...], wa_ref[...]) + _dot(yb_ref[...], wb_ref[...])


def _outproj(x, ys, ws):
    n, d = x.shape
    tm = min(512, n)
    kern = _outproj1_kernel if len(ys) == 1 else _outproj2_kernel
    in_specs = [pl.BlockSpec((tm, d), lambda i: (i, 0))]
    in_specs += [pl.BlockSpec((tm, y.shape[1]), lambda i: (i, 0)) for y in ys]
    in_specs += [pl.BlockSpec(w.shape, lambda i: (0, 0)) for w in ws]
    return pl.pallas_call(
        kern,
        grid=(n // tm,),
        in_specs=in_specs,
        out_specs=pl.BlockSpec((tm, d), lambda i: (i, 0)),
        out_shape=jax.ShapeDtypeStruct((n, d), F32),
        compiler_params=_params(("parallel",)),
        name="outproj",
    )(x, *ys, *ws)


def _rope_kernel(inv_ref, cos_ref, sin_ref):
    rows = cos_ref.shape[0]
    pos = (pl.program_id(0) * rows + lax.broadcasted_iota(jnp.int32, (rows, RET_DK), 0)).astype(F32)
    lane = lax.broadcasted_iota(jnp.int32, (rows, RET_DK), 1)
    ang = pos * inv_ref[...]
    cos_ref[...] = jnp.cos(ang)
    sin_ref[...] = jnp.where(lane < RET_DK // 2, -jnp.sin(ang), jnp.sin(ang))


def _rope_tables(t):
    half = RET_DK // 2
    inv = ROPE_BASE ** (-jnp.arange(half, dtype=F32) / half)
    inv = jnp.concatenate([inv, inv])[None, :]
    rows = min(512, t)
    return pl.pallas_call(
        _rope_kernel,
        grid=(t // rows,),
        in_specs=[pl.BlockSpec((1, RET_DK), lambda i: (0, 0))],
        out_specs=[pl.BlockSpec((rows, RET_DK), lambda i: (i, 0))] * 2,
        out_shape=[jax.ShapeDtypeStruct((t, RET_DK), F32)] * 2,
        compiler_params=_params(("parallel",)),
        name="rope_tables",
    )(inv)


def _even_core_kernel(p_ref, cos_ref, sin_ref, qn_ref, kn_ref, gn_ref, cw_ref, cb_ref, dtb_ref, alog_ref,
                      dsk_ref, sg_ref, e_ref, tril_ref, y_ref, rstate, sstate, xpad, decay_scr):
    t = pl.program_id(1)
    c = CHUNK
    row = lax.broadcasted_iota(jnp.int32, (c, c), 0)
    col = lax.broadcasted_iota(jnp.int32, (c, c), 1)
    causal = row >= col

    @pl.when(t == 0)
    def _init():
        rstate[...] = jnp.zeros_like(rstate)
        sstate[...] = jnp.zeros_like(sstate)
        xpad[0:CONV_PAD, :] = jnp.zeros((CONV_PAD, SSD_CONV_DIM), F32)
        rel = (row - col).astype(F32)
        for h in range(RET_HEADS):
            decay_scr[h] = jnp.where(causal, jnp.exp(RET_LOG_GAMMA[h] * jnp.maximum(rel, 0.0)), 0.0)

    @pl.when(t > 0)
    def _tail():
        xpad[0:CONV_PAD, :] = xpad[c:c + CONV_PAD, :]

    xpad[CONV_PAD:CONV_PAD + c, :] = p_ref[:, EV_XBC:EV_XBC + SSD_CONV_DIM]

    cos = cos_ref[...]
    sin = sin_ref[...]
    idx = lax.broadcasted_iota(jnp.int32, (c, 1), 0).astype(F32)
    for h in range(RET_HEADS):
        lg = RET_LOG_GAMMA[h]
        qh = _rms(p_ref[:, EV_Q + h * RET_DK:EV_Q + (h + 1) * RET_DK], qn_ref[...])
        kh = _rms(p_ref[:, EV_K + h * RET_DK:EV_K + (h + 1) * RET_DK], kn_ref[...])
        qh = qh * cos + pltpu.roll(qh, RET_DK // 2, 1) * sin
        kh = (kh * cos + pltpu.roll(kh, RET_DK // 2, 1) * sin) * (RET_DK ** -0.5)
        vb = p_ref[:, EV_V + h * RET_DV:EV_V + (h + 1) * RET_DV].astype(BF16)
        scores = _dot_nt(qh.astype(BF16), kh.astype(BF16)) * decay_scr[h]
        inner = _dot(scores.astype(BF16), vb)
        s_prev = rstate[h]
        q_dec = jnp.exp(lg * (idx + 1.0))
        cross = _dot((qh * q_dec).astype(BF16), s_prev.astype(BF16))
        k_dec = jnp.exp(lg * (float(c - 1) - idx))
        rstate[h] = s_prev * math.exp(lg * c) + _dot_tn((kh * k_dec).astype(BF16), vb)
        ya = inner + cross
        yc = ya - jnp.mean(ya, axis=-1, keepdims=True)
        yn = yc * lax.rsqrt(jnp.mean(yc * yc, axis=-1, keepdims=True) + EPS)
        yn = yn * gn_ref[:, h * RET_DV:(h + 1) * RET_DV]
        gate = _silu(p_ref[:, EV_G + h * RET_DV:EV_G + (h + 1) * RET_DV])
        y_ref[:, h * RET_DV:(h + 1) * RET_DV] = (gate * yn).astype(y_ref.dtype)

    taps = SSD_CONV
    xc = cb_ref[...]
    for k in range(taps):
        off = CONV_PAD - (taps - 1) + k
        xc = xc + cw_ref[k:k + 1, :] * xpad[off:off + c, :]
    xc = _silu(xc)
    xs = xc[:, :SSD_D_INNER]
    gs = SSD_GROUPS * SSD_STATE
    bm = xc[:, SSD_D_INNER:SSD_D_INNER + gs]
    cm = xc[:, SSD_D_INNER + gs:SSD_D_INNER + 2 * gs]

    dt = _softplus(p_ref[:, EV_DT:EV_DT + 128] + dtb_ref[...])
    a_neg = -jnp.exp(alog_ref[...])
    d_a = dt * a_neg
    acum = _dot_exact_rhs(tril_ref[...], d_a)
    acum_t = acum.T
    exp_a = jnp.exp(acum)
    dec = jnp.exp(acum[c - 1:c, :] - acum)
    e_mat = e_ref[...]
    dt_e = _dot_exact_lhs(dt, e_mat)
    exp_a_e = _dot_exact_lhs(exp_a, e_mat)
    dec_e = _dot_exact_lhs(dec, e_mat)
    xr = xs * dt_e
    xrd_b = (xr * dec_e).astype(BF16)
    cdec_e = exp_a_e[c - 1:c, :]
    lane = lax.broadcasted_iota(jnp.int32, (c, 128), 1)
    left = lane < SSD_HEADDIM
    hpg = SSD_HEADS // SSD_GROUPS
    gw = hpg * SSD_HEADDIM
    z_gate = _silu(p_ref[:, EV_Z:EV_Z + SSD_D_INNER])
    for g in range(SSD_GROUPS):
        bg = bm[:, g * SSD_STATE:(g + 1) * SSD_STATE].astype(BF16)
        cg = cm[:, g * SSD_STATE:(g + 1) * SSD_STATE].astype(BF16)
        cb_mat = _dot_nt(cg, bg)
        s_prev = sstate[g]
        y_off = _dot(cg, s_prev.astype(BF16)) * exp_a_e[:, g * gw:(g + 1) * gw]
        pairs = []
        for pr in range(hpg // 2):
            h0 = g * hpg + 2 * pr
            ms = []
            for h in (h0, h0 + 1):
                seg = acum[:, h:h + 1] - acum_t[h:h + 1, :]
                l_mat = jnp.where(causal, jnp.exp(jnp.minimum(seg, 0.0)), 0.0)
                ms.append((cb_mat * l_mat).astype(BF16))
            xp = xr[:, h0 * SSD_HEADDIM:(h0 + 2) * SSD_HEADDIM]
            xbd = jnp.concatenate([jnp.where(left, xp, 0.0), jnp.where(left, 0.0, xp)], axis=0).astype(BF16)
            pairs.append(_dot(jnp.concatenate(ms, axis=1), xbd))
        y_diag = jnp.concatenate(pairs, axis=1)
        sstate[g] = s_prev * cdec_e[:, g * gw:(g + 1) * gw] + _dot_tn(bg, xrd_b[:, g * gw:(g + 1) * gw])
        yb = y_diag + y_off + xs[:, g * gw:(g + 1) * gw] * dsk_ref[:, g * gw:(g + 1) * gw]
        yb = yb * z_gate[:, g * gw:(g + 1) * gw]
        yb = _rms(yb, sg_ref[:, g * gw:(g + 1) * gw])
        lo = RET_HEADS * RET_DV + g * gw
        y_ref[:, lo:lo + gw] = yb.astype(y_ref.dtype)


def _even_core(p, cos, sin, qn, kn, gn, cw, cb, dtb, alog, dsk, sg, e_mat, tril):
    b, t, _ = p.shape
    c = CHUNK
    full = lambda a: pl.BlockSpec(a.shape, lambda bi, ti: (0,) * a.ndim)
    smalls = (qn, kn, gn, cw, cb, dtb, alog, dsk, sg, e_mat, tril)
    return pl.pallas_call(
        _even_core_kernel,
        grid=(b, t // c),
        in_specs=[
            pl.BlockSpec((None, c, EVEN_IN_PAD), lambda bi, ti: (bi, ti, 0)),
            pl.BlockSpec((c, RET_DK), lambda bi, ti: (ti, 0)),
            pl.BlockSpec((c, RET_DK), lambda bi, ti: (ti, 0)),
        ] + [full(a) for a in smalls],
        out_specs=pl.BlockSpec((None, c, EVEN_OUT), lambda bi, ti: (bi, ti, 0)),
        out_shape=jax.ShapeDtypeStruct((b, t, EVEN_OUT), BF16),
        scratch_shapes=[
            pltpu.VMEM((RET_HEADS, RET_DK, RET_DV), F32),
            pltpu.VMEM((SSD_GROUPS, SSD_STATE, SSD_D_INNER // SSD_GROUPS), F32),
            pltpu.VMEM((c + CONV_PAD, SSD_CONV_DIM), F32),
            pltpu.VMEM((RET_HEADS, c, c), F32),
        ],
        compiler_params=_params(("parallel", "arbitrary")),
        name="even_core",
    )(p, cos, sin, *smalls)


def _odd_core_kernel(p_ref, cw_ref, cb_ref, wg_ref, ba_ref, bx_ref, lam_ref, qn_ref, kn_ref, mavg_ref,
                     yc_ref, q_ref, k_ref, v_ref, xpad, a_scr, b_scr, hcarry):
    t = pl.program_id(1)
    r = p_ref.shape[0]

    @pl.when(t == 0)
    def _init():
        xpad[0:CONV_PAD, :] = jnp.zeros((CONV_PAD, LRU_WIDTH), F32)
        hcarry[...] = jnp.zeros_like(hcarry)

    @pl.when(t > 0)
    def _tail():
        xpad[0:CONV_PAD, :] = xpad[r:r + CONV_PAD, :]

    xpad[CONV_PAD:CONV_PAD + r, :] = p_ref[:, OD_XC:OD_XC + LRU_WIDTH]
    xc = cb_ref[...]
    for k in range(LRU_CONV):
        off = CONV_PAD - (LRU_CONV - 1) + k
        xc = xc + cw_ref[k:k + 1, :] * xpad[off:off + r, :]

    rs, is_ = [], []
    for blk in range(LRU_BLOCKS):
        g = _dot(xc[:, blk * LRU_BLOCK:(blk + 1) * LRU_BLOCK].astype(BF16), wg_ref[blk])
        rs.append(g[:, :LRU_BLOCK])
        is_.append(g[:, LRU_BLOCK:])
    rg = _sigmoid(jnp.concatenate(rs, axis=1) + ba_ref[...])
    ig = _sigmoid(jnp.concatenate(is_, axis=1) + bx_ref[...])
    log_a = -LRU_C * rg * _softplus(-lam_ref[...])
    a_scr[...] = jnp.exp(log_a)
    b_scr[...] = jnp.sqrt(1.0 - jnp.exp(2.0 * log_a)) * (ig * xc)

    rowi = lax.broadcasted_iota(jnp.int32, (SUBLANES, LRU_WIDTH), 0)

    def body(gi, carry):
        off = pl.multiple_of(gi * SUBLANES, SUBLANES)
        a = a_scr[pl.ds(off, SUBLANES), :]
        bv = b_scr[pl.ds(off, SUBLANES), :]
        for s in (1, 2, 4):
            a_sh = jnp.where(rowi >= s, pltpu.roll(a, s, 0), 1.0)
            b_sh = jnp.where(rowi >= s, pltpu.roll(bv, s, 0), 0.0)
            bv = a * b_sh + bv
            a = a * a_sh
        h = a * carry + bv
        b_scr[pl.ds(off, SUBLANES), :] = h
        return jnp.broadcast_to(h[SUBLANES - 1:SUBLANES, :], (SUBLANES, LRU_WIDTH))

    hcarry[...] = lax.fori_loop(0, r // SUBLANES, body, hcarry[...])
    yc_ref[...] = (b_scr[...] * _gelu_tanh(p_ref[:, OD_GATE:OD_GATE + LRU_WIDTH])).astype(yc_ref.dtype)

    def headnorm(x, g):
        sq = x * x
        hi = sq.astype(BF16)
        lo = (sq - hi.astype(F32)).astype(BF16)
        ms = _dot(hi, mavg_ref[...]) + _dot(lo, mavg_ref[...])
        return x * lax.rsqrt(ms + EPS) * g

    q = headnorm(p_ref[:, OD_Q:OD_Q + SB_WIDTH], qn_ref[...]) * (SB_HEAD_DIM ** -0.5)
    q_ref[...] = q.astype(q_ref.dtype)
    k_ref[...] = headnorm(p_ref[:, OD_K:OD_K + SB_WIDTH], kn_ref[...]).astype(k_ref.dtype)
    v_ref[...] = p_ref[:, OD_V:OD_V + SB_WIDTH].astype(v_ref.dtype)


def _odd_core(p, cw, cb, wg, ba, bx, lam, qn, kn, mavg):
    b, t, _ = p.shape
    r = min(256, t)
    full = lambda a: pl.BlockSpec(a.shape, lambda bi, ti: (0,) * a.ndim)
    smalls = (cw, cb, wg, ba, bx, lam, qn, kn, mavg)
    blk = lambda w: pl.BlockSpec((None, r, w), lambda bi, ti: (bi, ti, 0))
    return pl.pallas_call(
        _odd_core_kernel,
        grid=(b, t // r),
        in_specs=[blk(ODD_IN)] + [full(a) for a in smalls],
        out_specs=[blk(LRU_WIDTH), blk(SB_WIDTH), blk(SB_WIDTH), blk(SB_WIDTH)],
        out_shape=[jax.ShapeDtypeStruct((b, t, LRU_WIDTH), BF16)] + [jax.ShapeDtypeStruct((b, t, SB_WIDTH), BF16)] * 3,
        scratch_shapes=[
            pltpu.VMEM((r + CONV_PAD, LRU_WIDTH), F32),
            pltpu.VMEM((r, LRU_WIDTH), F32),
            pltpu.VMEM((r, LRU_WIDTH), F32),
            pltpu.VMEM((SUBLANES, LRU_WIDTH), F32),
        ],
        compiler_params=_params(("parallel", "arbitrary")),
        name="odd_core",
    )(p, *smalls)


SB_TQ = 256
SB_TK = 128


def _sb_attn_kernel(q_ref, k_ref, v_ref, uo_ref, o_ref, carry, acc):
    qi = pl.program_id(2)
    tq = q_ref.shape[0]
    tk = SB_TK
    lane_q = lax.broadcasted_iota(jnp.int32, (tq, 2 * SB_HEAD_DIM), 1)
    q = q_ref[...].astype(F32)
    qm = jnp.concatenate([jnp.where(lane_q < SB_HEAD_DIM, q, 0.0), jnp.where(lane_q < SB_HEAD_DIM, 0.0, q)], axis=0)
    qm = qm.astype(BF16)
    carry[...] = jnp.zeros_like(carry)
    acc[...] = jnp.zeros_like(acc)
    lane_k = lax.broadcasted_iota(jnp.int32, (tk, 2 * SB_HEAD_DIM), 1)
    rows = lax.broadcasted_iota(jnp.int32, (2 * tq, tk), 0)
    qpos = qi * tq + jnp.where(rows >= tq, rows - tq, rows)
    kcol = lax.broadcasted_iota(jnp.int32, (2 * tq, tk), 1)

    def tile(j, masked):
        start = pl.multiple_of(j * tk, tk)
        kb = k_ref[pl.ds(start, tk), :]
        z = _dot_nt(qm, kb)
        sp = jnp.log(1.0 + jnp.exp(-jnp.abs(z)))
        lb = jnp.minimum(z, 0.0) - sp
        l1 = lb - z
        if masked:
            valid = (start + kcol) < qpos
            l1 = jnp.where(valid, l1, 0.0)
        st = _dot(l1.astype(BF16), uo_ref[...])
        c_old = carry[...]
        w = jnp.exp(lb + st[:, :tk] + c_old)
        if masked:
            w = jnp.where(valid, w, 0.0)
        carry[...] = c_old + st[:, tk:]
        wb = w.astype(BF16)
        wcat = jnp.concatenate([wb[:tq], wb[tq:]], axis=1)
        vb = v_ref[pl.ds(start, tk), :].astype(F32)
        vbd = jnp.concatenate([jnp.where(lane_k < SB_HEAD_DIM, vb, 0.0), jnp.where(lane_k < SB_HEAD_DIM, 0.0, vb)], axis=0)
        acc[...] += _dot(wcat, vbd.astype(BF16))

    ndiag = tq // tk
    for d in range(ndiag):
        tile(qi * ndiag + (ndiag - 1 - d), True)

    def body(jj, _):
        tile(qi * ndiag - 1 - jj, False)
        return 0

    lax.fori_loop(0, qi * ndiag, body, 0)
    o_ref[...] = acc[...].astype(o_ref.dtype)


def _sb_attention(q, k, v, uo):
    b, t, _ = q.shape
    tq = min(SB_TQ, t)
    pairs = SB_HEADS // 2
    pw = 2 * SB_HEAD_DIM
    return pl.pallas_call(
        _sb_attn_kernel,
        grid=(b, pairs, t // tq),
        in_specs=[
            pl.BlockSpec((None, tq, pw), lambda bi, pi, qi: (bi, qi, pi)),
            pl.BlockSpec((None, t, pw), lambda bi, pi, qi: (bi, 0, pi)),
            pl.BlockSpec((None, t, pw), lambda bi, pi, qi: (bi, 0, pi)),
            pl.BlockSpec(uo.shape, lambda bi, pi, qi: (0, 0)),
        ],
        out_specs=pl.BlockSpec((None, tq, pw), lambda bi, pi, qi: (bi, qi, pi)),
        out_shape=jax.ShapeDtypeStruct((b, t, SB_WIDTH), BF16),
        scratch_shapes=[pltpu.VMEM((2 * tq, SB_TK), F32), pltpu.VMEM((tq, pw), F32)],
        compiler_params=_params(("parallel", "parallel", "arbitrary")),
        name="sb_attention",
    )(q, k, v, uo)


def _const_tables():
    i = np.arange(128)
    tril = (i[None, :] <= i[:, None]).astype(np.float32)
    e_mat = np.zeros((128, SSD_D_INNER), np.float32)
    for h in range(SSD_HEADS):
        e_mat[h, h * SSD_HEADDIM:(h + 1) * SSD_HEADDIM] = 1.0
    strict = (i[:, None] > i[None, :]).astype(np.float32)
    uo = np.concatenate([strict, np.ones((128, 128), np.float32)], axis=1)
    hd = np.arange(SB_WIDTH) // SB_HEAD_DIM
    mavg = (hd[:, None] == hd[None, :]).astype(np.float32) / SB_HEAD_DIM
    return (jnp.asarray(tril, BF16), jnp.asarray(e_mat, BF16), jnp.asarray(uo, BF16), jnp.asarray(mavg, BF16))


def _row(v):
    return v.reshape(1, -1).astype(F32)


def _pad_lanes(v, width=128):
    return jnp.pad(v.reshape(1, -1).astype(F32), ((0, 0), (0, width - v.shape[-1])))


def kernel(x, norm_mix, norm_mlp, mlp_w1, mlp_w2, ev_w_in, ev_w_out, ret_qn, ret_kn, ret_gn, ssd_conv_w, ssd_conv_b, ssd_dt_bias, ssd_a_log, ssd_d, ssd_norm, od_w_in, od_w_out, lru_conv_w, lru_conv_b, lru_wa, lru_ba, lru_wx, lru_bx, lru_lam, sb_qn, sb_kn):
    b, t, d = x.shape
    n = b * t
    depth = norm_mix.shape[0]
    tril, e_mat, uo, mavg = _const_tables()
    cos, sin = _rope_tables(t)
    xf = x.reshape(n, d)
    for l in range(depth):
        g_mix = _row(norm_mix[l])
        if l % 2 == 0:
            e = l // 2
            w_in = jnp.pad(ev_w_in[e], ((0, 0), (0, EVEN_IN_PAD - EVEN_IN))).astype(BF16)
            p = _norm_matmul(xf, g_mix, w_in, tn=640).reshape(b, t, EVEN_IN_PAD)
            y = _even_core(
                p, cos, sin, _row(ret_qn[e]), _row(ret_kn[e]), _row(ret_gn[e]),
                ssd_conv_w[e].astype(F32), _row(ssd_conv_b[e]), _pad_lanes(ssd_dt_bias[e]), _pad_lanes(ssd_a_log[e]),
                _row(jnp.repeat(ssd_d[e], SSD_HEADDIM)), _row(ssd_norm[e]), e_mat, tril)
            xf = _outproj(xf, [y.reshape(n, EVEN_OUT)], [ev_w_out[e].astype(BF16)])
        else:
            o = l // 2
            p = _norm_matmul(xf, g_mix, od_w_in[o].astype(BF16), tn=512).reshape(b, t, ODD_IN)
            wg = jnp.concatenate([lru_wa[o], lru_wx[o]], axis=-1).astype(BF16)
            yc, q, k, v = _odd_core(
                p, lru_conv_w[o].astype(F32), _row(lru_conv_b[o]), wg, _row(lru_ba[o]), _row(lru_bx[o]),
                _row(lru_lam[o]), _row(jnp.tile(sb_qn[o], SB_HEADS)), _row(jnp.tile(sb_kn[o], SB_HEADS)), mavg)
            yd = _sb_attention(q, k, v, uo)
            w_out = od_w_out[o].astype(BF16)
            xf = _outproj(xf, [yc.reshape(n, LRU_WIDTH), yd.reshape(n, SB_WIDTH)], [w_out[:LRU_WIDTH], w_out[LRU_WIDTH:]])
        xf = _mlp(xf, _row(norm_mlp[l]), mlp_w1[l].astype(BF16), mlp_w2[l].astype(BF16))
    return xf.reshape(b, t, d)
```

```python
import functools
import math

import numpy as np
import jax
import jax.numpy as jnp
from jax import lax
from jax.experimental import pallas as pl
from jax.experimental.pallas import tpu as pltpu

F32 = jnp.float32
BF16 = jnp.bfloat16

D_MODEL = 1024
EPS = 1e-6

RET_HEADS = 4
RET_DK = 128
RET_DV = 256
ROPE_BASE = 10000.0
RET_LOG_GAMMA = tuple(float(np.log1p(-(2.0 ** (-5.0 - h)))) for h in range(RET_HEADS))

SSD_D_INNER = 1024
SSD_HEADDIM = 64
SSD_HEADS = 16
SSD_GROUPS = 2
SSD_STATE = 128
SSD_CONV = 4
SSD_CONV_DIM = SSD_D_INNER + 2 * SSD_GROUPS * SSD_STATE
CHUNK = 128

LRU_WIDTH = 1024
LRU_BLOCKS = 8
LRU_BLOCK = 128
LRU_C = 8.0
LRU_CONV = 4

SB_HEADS = 8
SB_HEAD_DIM = 64
SB_WIDTH = SB_HEADS * SB_HEAD_DIM

D_FF = 4 * D_MODEL

EVEN_IN = 5648
EVEN_IN_PAD = 5760
EVEN_OUT = 2048
ODD_IN = 3584
ODD_OUT = 1536

EV_Q, EV_K, EV_V, EV_G, EV_Z, EV_XBC, EV_DT = 0, 512, 1024, 2048, 3072, 4096, 5632
OD_GATE, OD_XC, OD_Q, OD_K, OD_V = 0, 1024, 2048, 2560, 3072

LOG2E = 1.4426950408889634
SIGN_BIT = np.int32(-2 ** 31)
SUBLANES = 8
CONV_PAD = 8
VMEM_LIMIT = 56 * 1024 * 1024


def _params(sem):
    return pltpu.CompilerParams(dimension_semantics=sem, vmem_limit_bytes=VMEM_LIMIT)


def _dot(a, b):
    return jnp.dot(a, b, preferred_element_type=F32)


def _dot_nt(a, b):
    return lax.dot_general(a, b, (((1,), (1,)), ((), ())), preferred_element_type=F32)


def _dot_tn(a, b):
    return lax.dot_general(a, b, (((0,), (0,)), ((), ())), preferred_element_type=F32)


def _split3(x):
    hi = x.astype(BF16)
    r = x - hi.astype(F32)
    mid = r.astype(BF16)
    lo = (r - mid.astype(F32)).astype(BF16)
    return hi, mid, lo


def _dot_exact_lhs(x, m):
    hi, mid, lo = _split3(x)
    return _dot(hi, m) + _dot(mid, m) + _dot(lo, m)


def _dot_exact_rhs(m, x):
    hi, mid, lo = _split3(x)
    return _dot(m, hi) + _dot(m, mid) + _dot(m, lo)


def _sigmoid(x):
    return 1.0 / (1.0 + jnp.exp(-x))


def _silu(x):
    return x * _sigmoid(x)


def _softplus(x):
    return jnp.maximum(x, 0.0) + jnp.log(1.0 + jnp.exp(-jnp.abs(x)))


def _gelu_tanh(x):
    c = math.sqrt(2.0 / math.pi)
    return x * (0.5 * (1.0 + jnp.tanh(c * (x + 0.044715 * (x * x * x)))))


def _rms(x, g):
    return x * lax.rsqrt(jnp.mean(x * x, axis=-1, keepdims=True) + EPS) * g


def _norm_matmul_kernel(x_ref, g_ref, w_ref, o_ref, h_scr):
    @pl.when(pl.program_id(1) == 0)
    def _():
        h_scr[...] = _rms(x_ref[...], g_ref[...]).astype(BF16)

    o_ref[...] = _dot(h_scr[...], w_ref[...]).astype(o_ref.dtype)


def _norm_matmul(x, g, w, tn, out_dtype=F32):
    n, d = x.shape
    n_out = w.shape[1]
    tm = min(1024, n)
    return pl.pallas_call(
        _norm_matmul_kernel,
        grid=(n // tm, n_out // tn),
        in_specs=[
            pl.BlockSpec((tm, d), lambda i, j: (i, 0)),
            pl.BlockSpec((1, d), lambda i, j: (0, 0)),
            pl.BlockSpec((d, tn), lambda i, j: (0, j)),
        ],
        out_specs=pl.BlockSpec((tm, tn), lambda i, j: (i, j)),
        out_shape=jax.ShapeDtypeStruct((n, n_out), out_dtype),
        scratch_shapes=[pltpu.VMEM((tm, d), BF16)],
        compiler_params=_params(("parallel", "arbitrary")),
        name="norm_matmul",
    )(x, g, w)


def _mlp_kernel(x_ref, g_ref, w1_ref, w2_ref, o_ref, h_scr, acc_scr):
    f = pl.program_id(1)

    @pl.when(f == 0)
    def _():
        h_scr[...] = _rms(x_ref[...], g_ref[...]).astype(BF16)
        acc_scr[...] = jnp.zeros_like(acc_scr)

    a = _dot(h_scr[...], w1_ref[...])
    a = jnp.square(jnp.maximum(a, 0.0)).astype(BF16)
    acc_scr[...] += _dot(a, w2_ref[...])

    @pl.when(f == pl.num_programs(1) - 1)
    def _():
        o_ref[...] = x_ref[...] + acc_scr[...]


def _mlp(x, g, w1, w2):
    n, d = x.shape
    dff = w1.shape[1]
    tm = min(1024, n)
    tf = 512
    return pl.pallas_call(
        _mlp_kernel,
        grid=(n // tm, dff // tf),
        in_specs=[
            pl.BlockSpec((tm, d), lambda i, f: (i, 0)),
            pl.BlockSpec((1, d), lambda i, f: (0, 0)),
            pl.BlockSpec((d, tf), lambda i, f: (0, f)),
            pl.BlockSpec((tf, d), lambda i, f: (f, 0)),
        ],
        out_specs=pl.BlockSpec((tm, d), lambda i, f: (i, 0)),
        out_shape=jax.ShapeDtypeStruct((n, d), F32),
        scratch_shapes=[pltpu.VMEM((tm, d), BF16), pltpu.VMEM((tm, d), F32)],
        compiler_params=_params(("parallel", "arbitrary")),
        name="mlp",
    )(x, g, w1, w2)


def _outproj1_kernel(x_ref, y_ref, w_ref, o_ref):
    o_ref[...] = x_ref[...] + _dot(y_ref[...], w_ref[...])


def _outproj2_kernel(x_ref, ya_ref, yb_ref, wa_ref, wb_ref, o_ref):
    o_ref[...] = x_ref[...] + _dot(ya_ref[...], wa_ref[...]) + _dot(yb_ref[...], wb_ref[...])


def _outproj(x, ys, ws):
    n, d = x.shape
    tm = min(512, n)
    kern = _outproj1_kernel if len(ys) == 1 else _outproj2_kernel
    in_specs = [pl.BlockSpec((tm, d), lambda i: (i, 0))]
    in_specs += [pl.BlockSpec((tm, y.shape[1]), lambda i: (i, 0)) for y in ys]
    in_specs += [pl.BlockSpec(w.shape, lambda i: (0, 0)) for w in ws]
    return pl.pallas_call(
        kern,
        grid=(n // tm,),
        in_specs=in_specs,
        out_specs=pl.BlockSpec((tm, d), lambda i: (i, 0)),
        out_shape=jax.ShapeDtypeStruct((n, d), F32),
        compiler_params=_params(("parallel",)),
        name="outproj",
    )(x, *ys, *ws)


def _rope_kernel(inv_ref, cos_ref, sin_ref):
    rows = cos_ref.shape[0]
    pos = (pl.program_id(0) * rows + lax.broadcasted_iota(jnp.int32, (rows, RET_DK), 0)).astype(F32)
    lane = lax.broadcasted_iota(jnp.int32, (rows, RET_DK), 1)
    ang = pos * inv_ref[...]
    cos_ref[...] = jnp.cos(ang)
    sin_ref[...] = jnp.where(lane < RET_DK // 2, -jnp.sin(ang), jnp.sin(ang))


def _rope_tables(t):
    half = RET_DK // 2
    inv = ROPE_BASE ** (-jnp.arange(half, dtype=F32) / half)
    inv = jnp.concatenate([inv, inv])[None, :]
    rows = min(512, t)
    return pl.pallas_call(
        _rope_kernel,
        grid=(t // rows,),
        in_specs=[pl.BlockSpec((1, RET_DK), lambda i: (0, 0))],
        out_specs=[pl.BlockSpec((rows, RET_DK), lambda i: (i, 0))] * 2,
        out_shape=[jax.ShapeDtypeStruct((t, RET_DK), F32)] * 2,
        compiler_params=_params(("parallel",)),
        name="rope_tables",
    )(inv)


def _even_core_kernel(p_ref, cos_ref, sin_ref, qn_ref, kn_ref, gn_ref, cw_ref, cb_ref, dtb_ref, alog_ref,
                      dsk_ref, sg_ref, e_ref, tril_ref, y_ref, rstate, sstate, xpad, decay_scr):
    t = pl.program_id(1)
    c = CHUNK
    row = lax.broadcasted_iota(jnp.int32, (c, c), 0)
    col = lax.broadcasted_iota(jnp.int32, (c, c), 1)
    causal = row >= col

    @pl.when(t == 0)
    def _init():
        rstate[...] = jnp.zeros_like(rstate)
        sstate[...] = jnp.zeros_like(sstate)
        xpad[0:CONV_PAD, :] = jnp.zeros((CONV_PAD, SSD_CONV_DIM), F32)
        rel = (row - col).astype(F32)
        for h in range(RET_HEADS):
            decay_scr[h] = jnp.where(causal, jnp.exp(RET_LOG_GAMMA[h] * jnp.maximum(rel, 0.0)), 0.0)

    @pl.when(t > 0)
    def _tail():
        xpad[0:CONV_PAD, :] = xpad[c:c + CONV_PAD, :]

    xpad[CONV_PAD:CONV_PAD + c, :] = p_ref[:, EV_XBC:EV_XBC + SSD_CONV_DIM]

    cos = cos_ref[...]
    sin = sin_ref[...]
    idx = lax.broadcasted_iota(jnp.int32, (c, 1), 0).astype(F32)
    for h in range(RET_HEADS):
        lg = RET_LOG_GAMMA[h]
        qh = _rms(p_ref[:, EV_Q + h * RET_DK:EV_Q + (h + 1) * RET_DK], qn_ref[...])
        kh = _rms(p_ref[:, EV_K + h * RET_DK:EV_K + (h + 1) * RET_DK], kn_ref[...])
        qh = qh * cos + pltpu.roll(qh, RET_DK // 2, 1) * sin
        kh = (kh * cos + pltpu.roll(kh, RET_DK // 2, 1) * sin) * (RET_DK ** -0.5)
        vb = p_ref[:, EV_V + h * RET_DV:EV_V + (h + 1) * RET_DV].astype(BF16)
        scores = _dot_nt(qh.astype(BF16), kh.astype(BF16)) * decay_scr[h]
        inner = _dot(scores.astype(BF16), vb)
        s_prev = rstate[h]
        q_dec = jnp.exp(lg * (idx + 1.0))
        cross = _dot((qh * q_dec).astype(BF16), s_prev.astype(BF16))
        k_dec = jnp.exp(lg * (float(c - 1) - idx))
        rstate[h] = s_prev * math.exp(lg * c) + _dot_tn((kh * k_dec).astype(BF16), vb)
        ya = inner + cross
        yc = ya - jnp.mean(ya, axis=-1, keepdims=True)
        yn = yc * lax.rsqrt(jnp.mean(yc * yc, axis=-1, keepdims=True) + EPS)
        yn = yn * gn_ref[:, h * RET_DV:(h + 1) * RET_DV]
        gate = _silu(p_ref[:, EV_G + h * RET_DV:EV_G + (h + 1) * RET_DV])
        y_ref[:, h * RET_DV:(h + 1) * RET_DV] = (gate * yn).astype(y_ref.dtype)

    taps = SSD_CONV
    xc = cb_ref[...]
    for k in range(taps):
        off = CONV_PAD - (taps - 1) + k
        xc = xc + cw_ref[k:k + 1, :] * xpad[off:off + c, :]
    xc = _silu(xc)
    xs = xc[:, :SSD_D_INNER]
    gs = SSD_GROUPS * SSD_STATE
    bm = xc[:, SSD_D_INNER:SSD_D_INNER + gs]
    cm = xc[:, SSD_D_INNER + gs:SSD_D_INNER + 2 * gs]

    dt = _softplus(p_ref[:, EV_DT:EV_DT + 128] + dtb_ref[...])
    a_neg = -jnp.exp(alog_ref[...])
    d_a = dt * a_neg
    acum = _dot_exact_rhs(tril_ref[...], d_a)
    acum_t = acum.T
    exp_a = jnp.exp(acum)
    dec = jnp.exp(acum[c - 1:c, :] - acum)
    e_mat = e_ref[...]
    dt_e = _dot_exact_lhs(dt, e_mat)
    exp_a_e = _dot_exact_lhs(exp_a, e_mat)
    dec_e = _dot_exact_lhs(dec, e_mat)
    xr = xs * dt_e
    xrd_b = (xr * dec_e).astype(BF16)
    cdec_e = exp_a_e[c - 1:c, :]
    lane = lax.broadcasted_iota(jnp.int32, (c, 128), 1)
    left = lane < SSD_HEADDIM
    hpg = SSD_HEADS // SSD_GROUPS
    gw = hpg * SSD_HEADDIM
    z_gate = _silu(p_ref[:, EV_Z:EV_Z + SSD_D_INNER])
    for g in range(SSD_GROUPS):
        bg = bm[:, g * SSD_STATE:(g + 1) * SSD_STATE].astype(BF16)
        cg = cm[:, g * SSD_STATE:(g + 1) * SSD_STATE].astype(BF16)
        cb_mat = _dot_nt(cg, bg)
        s_prev = sstate[g]
        y_off = _dot(cg, s_prev.astype(BF16)) * exp_a_e[:, g * gw:(g + 1) * gw]
        pairs = []
        for pr in range(hpg // 2):
            h0 = g * hpg + 2 * pr
            ms = []
            for h in (h0, h0 + 1):
                seg = acum[:, h:h + 1] - acum_t[h:h + 1, :]
                l_mat = jnp.where(causal, jnp.exp(jnp.minimum(seg, 0.0)), 0.0)
                ms.append((cb_mat * l_mat).astype(BF16))
            xp = xr[:, h0 * SSD_HEADDIM:(h0 + 2) * SSD_HEADDIM]
            xbd = jnp.concatenate([jnp.where(left, xp, 0.0), jnp.where(left, 0.0, xp)], axis=0).astype(BF16)
            pairs.append(_dot(jnp.concatenate(ms, axis=1), xbd))
        y_diag = jnp.concatenate(pairs, axis=1)
        sstate[g] = s_prev * cdec_e[:, g * gw:(g + 1) * gw] + _dot_tn(bg, xrd_b[:, g * gw:(g + 1) * gw])
        yb = y_diag + y_off + xs[:, g * gw:(g + 1) * gw] * dsk_ref[:, g * gw:(g + 1) * gw]
        yb = yb * z_gate[:, g * gw:(g + 1) * gw]
        yb = _rms(yb, sg_ref[:, g * gw:(g + 1) * gw])
        lo = RET_HEADS * RET_DV + g * gw
        y_ref[:, lo:lo + gw] = yb.astype(y_ref.dtype)


def _even_core(p, cos, sin, qn, kn, gn, cw, cb, dtb, alog, dsk, sg, e_mat, tril):
    b, t, _ = p.shape
    c = CHUNK
    full = lambda a: pl.BlockSpec(a.shape, lambda bi, ti: (0,) * a.ndim)
    smalls = (qn, kn, gn, cw, cb, dtb, alog, dsk, sg, e_mat, tril)
    return pl.pallas_call(
        _even_core_kernel,
        grid=(b, t // c),
        in_specs=[
            pl.BlockSpec((None, c, EVEN_IN_PAD), lambda bi, ti: (bi, ti, 0)),
            pl.BlockSpec((c, RET_DK), lambda bi, ti: (ti, 0)),
            pl.BlockSpec((c, RET_DK), lambda bi, ti: (ti, 0)),
        ] + [full(a) for a in smalls],
        out_specs=pl.BlockSpec((None, c, EVEN_OUT), lambda bi, ti: (bi, ti, 0)),
        out_shape=jax.ShapeDtypeStruct((b, t, EVEN_OUT), BF16),
        scratch_shapes=[
            pltpu.VMEM((RET_HEADS, RET_DK, RET_DV), F32),
            pltpu.VMEM((SSD_GROUPS, SSD_STATE, SSD_D_INNER // SSD_GROUPS), F32),
            pltpu.VMEM((c + CONV_PAD, SSD_CONV_DIM), F32),
            pltpu.VMEM((RET_HEADS, c, c), F32),
        ],
        compiler_params=_params(("parallel", "arbitrary")),
        name="even_core",
    )(p, cos, sin, *smalls)


def _odd_core_kernel(p_ref, cw_ref, cb_ref, wg_ref, ba_ref, bx_ref, lam_ref, qn_ref, kn_ref, mavg_ref,
                     yc_ref, q_ref, k_ref, v_ref, xpad, a_scr, b_scr, hcarry):
    t = pl.program_id(1)
    r = p_ref.shape[0]

    @pl.when(t == 0)
    def _init():
        xpad[0:CONV_PAD, :] = jnp.zeros((CONV_PAD, LRU_WIDTH), F32)
        hcarry[...] = jnp.zeros_like(hcarry)

    @pl.when(t > 0)
    def _tail():
        xpad[0:CONV_PAD, :] = xpad[r:r + CONV_PAD, :]

    xpad[CONV_PAD:CONV_PAD + r, :] = p_ref[:, OD_XC:OD_XC + LRU_WIDTH]
    xc = cb_ref[...]
    for k in range(LRU_CONV):
        off = CONV_PAD - (LRU_CONV - 1) + k
        xc = xc + cw_ref[k:k + 1, :] * xpad[off:off + r, :]

    rs, is_ = [], []
    for blk in range(LRU_BLOCKS):
        g = _dot(xc[:, blk * LRU_BLOCK:(blk + 1) * LRU_BLOCK].astype(BF16), wg_ref[blk])
        rs.append(g[:, :LRU_BLOCK])
        is_.append(g[:, LRU_BLOCK:])
    rg = _sigmoid(jnp.concatenate(rs, axis=1) + ba_ref[...])
    ig = _sigmoid(jnp.concatenate(is_, axis=1) + bx_ref[...])
    log_a = -LRU_C * rg * _softplus(-lam_ref[...])
    a_scr[...] = jnp.exp(log_a)
    b_scr[...] = jnp.sqrt(1.0 - jnp.exp(2.0 * log_a)) * (ig * xc)

    rowi = lax.broadcasted_iota(jnp.int32, (SUBLANES, LRU_WIDTH), 0)

    def body(gi, carry):
        off = pl.multiple_of(gi * SUBLANES, SUBLANES)
        a = a_scr[pl.ds(off, SUBLANES), :]
        bv = b_scr[pl.ds(off, SUBLANES), :]
        for s in (1, 2, 4):
            a_sh = jnp.where(rowi >= s, pltpu.roll(a, s, 0), 1.0)
            b_sh = jnp.where(rowi >= s, pltpu.roll(bv, s, 0), 0.0)
            bv = a * b_sh + bv
            a = a * a_sh
        h = a * carry + bv
        b_scr[pl.ds(off, SUBLANES), :] = h
        return jnp.broadcast_to(h[SUBLANES - 1:SUBLANES, :], (SUBLANES, LRU_WIDTH))

    hcarry[...] = lax.fori_loop(0, r // SUBLANES, body, hcarry[...])
    yc_ref[...] = (b_scr[...] * _gelu_tanh(p_ref[:, OD_GATE:OD_GATE + LRU_WIDTH])).astype(yc_ref.dtype)

    def headnorm(x, g):
        sq = x * x
        hi = sq.astype(BF16)
        lo = (sq - hi.astype(F32)).astype(BF16)
        ms = _dot(hi, mavg_ref[...]) + _dot(lo, mavg_ref[...])
        return x * lax.rsqrt(ms + EPS) * g

    q = headnorm(p_ref[:, OD_Q:OD_Q + SB_WIDTH], qn_ref[...]) * (SB_HEAD_DIM ** -0.5 * LOG2E)
    q_ref[...] = q.astype(q_ref.dtype)
    kn = headnorm(p_ref[:, OD_K:OD_K + SB_WIDTH], kn_ref[...])
    vt = p_ref[:, OD_V:OD_V + SB_WIDTH].T
    pw = 2 * SB_HEAD_DIM
    lane = lax.broadcasted_iota(jnp.int32, (r, pw), 1)
    sub = lax.broadcasted_iota(jnp.int32, (pw, r), 0)
    for pr in range(SB_HEADS // 2):
        kp = kn[:, pr * pw:(pr + 1) * pw]
        k_ref[:, 2 * pr * pw:(2 * pr + 1) * pw] = jnp.where(lane < SB_HEAD_DIM, kp, 0.0).astype(k_ref.dtype)
        k_ref[:, (2 * pr + 1) * pw:(2 * pr + 2) * pw] = jnp.where(lane < SB_HEAD_DIM, 0.0, kp).astype(k_ref.dtype)
        vp = vt[pr * pw:(pr + 1) * pw, :]
        v_ref[2 * pr * pw:(2 * pr + 1) * pw, :] = jnp.where(sub < SB_HEAD_DIM, vp, 0.0).astype(v_ref.dtype)
        v_ref[(2 * pr + 1) * pw:(2 * pr + 2) * pw, :] = jnp.where(sub < SB_HEAD_DIM, 0.0, vp).astype(v_ref.dtype)


def _odd_core(p, cw, cb, wg, ba, bx, lam, qn, kn, mavg):
    b, t, _ = p.shape
    r = min(256, t)
    full = lambda a: pl.BlockSpec(a.shape, lambda bi, ti: (0,) * a.ndim)
    smalls = (cw, cb, wg, ba, bx, lam, qn, kn, mavg)
    blk = lambda w: pl.BlockSpec((None, r, w), lambda bi, ti: (bi, ti, 0))
    return pl.pallas_call(
        _odd_core_kernel,
        grid=(b, t // r),
        in_specs=[blk(ODD_IN)] + [full(a) for a in smalls],
        out_specs=[blk(LRU_WIDTH), blk(SB_WIDTH), blk(2 * SB_WIDTH),
                   pl.BlockSpec((None, 2 * SB_WIDTH, r), lambda bi, ti: (bi, 0, ti))],
        out_shape=[jax.ShapeDtypeStruct((b, t, LRU_WIDTH), BF16), jax.ShapeDtypeStruct((b, t, SB_WIDTH), BF16),
                   jax.ShapeDtypeStruct((b, t, 2 * SB_WIDTH), BF16), jax.ShapeDtypeStruct((b, 2 * SB_WIDTH, t), BF16)],
        scratch_shapes=[
            pltpu.VMEM((r + CONV_PAD, LRU_WIDTH), F32),
            pltpu.VMEM((r, LRU_WIDTH), F32),
            pltpu.VMEM((r, LRU_WIDTH), F32),
            pltpu.VMEM((SUBLANES, LRU_WIDTH), F32),
        ],
        compiler_params=_params(("parallel", "arbitrary")),
        name="odd_core",
    )(p, *smalls)


SB_TQ = 512
SB_TK = 128
SB_CARRY_ROWS = 16
SB_STAGES = 5


def _sb_attn_kernel(q_ref, k_ref, vt_ref, ut_ref, o_ref, z_buf, lb_buf, l1b_buf, r0_buf, r0_prev, sfx_buf, w_buf, acc):
    qi = pl.program_id(2)
    tq = q_ref.shape[0]
    tk = SB_TK
    pw = 2 * SB_HEAD_DIM
    ndiag = tq // tk
    jmax = qi * ndiag + (ndiag - 1)
    kloc = lax.broadcasted_iota(jnp.int32, (tk, tq), 0)
    qpos = qi * tq + lax.broadcasted_iota(jnp.int32, (tk, tq), 1)
    rid = lax.broadcasted_iota(jnp.int32, (SB_CARRY_ROWS, 2 * tq), 0)
    kpad = jnp.zeros((tk - SB_CARRY_ROWS, 2 * tq), BF16)

    def key_start(t):
        return pl.multiple_of((jmax - t) * tk, tk)

    def p0(t):
        kx = k_ref[pl.ds(key_start(t), tk), :]
        km = jnp.concatenate([kx[:, :pw], kx[:, pw:]], axis=0)
        z_buf[...] = _dot_nt(km, q_ref[...])

    def p1(t, slot, masked):
        if masked:
            valid = (key_start(t) + kloc) < qpos
        for h in range(2):
            z = z_buf[h * tk:(h + 1) * tk, :]
            neg_abs = lax.bitcast_convert_type(lax.bitcast_convert_type(z, jnp.int32) | SIGN_BIT, F32)
            sp = jnp.log(1.0 + jnp.exp2(neg_abs)) * LOG2E
            lb = jnp.minimum(z, 0.0) - sp
            l1 = lb - z
            if masked:
                l1 = jnp.where(valid, l1, 0.0)
            lb_buf[slot, h * tk:(h + 1) * tk, :] = lb
            l1b_buf[:, h * tq:(h + 1) * tq] = l1.astype(BF16)
            r0_buf[:, h * tq:(h + 1) * tq] = l1[0:1, :]

    def p2(first):
        if first:
            c_row = jnp.zeros((1, 2 * tq), F32)
        else:
            c_row = sfx_buf[0:1, :] + r0_prev[...]
        c_hi = c_row.astype(BF16).astype(F32)
        c_r = c_row - c_hi
        c_mid = c_r.astype(BF16).astype(F32)
        c_lo = c_r - c_mid
        cext = jnp.where(rid == 0, c_hi, jnp.where(rid == 1, c_mid, jnp.where(rid == 2, c_lo, 0.0))).astype(BF16)
        rhs = jnp.concatenate([l1b_buf[...], cext, kpad], axis=0)
        sfx_buf[...] = _dot(ut_ref[...], rhs)
        r0_prev[...] = r0_buf[...]

    def p3(t, slot, masked):
        if masked:
            valid = (key_start(t) + kloc) < qpos
        for h in range(2):
            w = jnp.exp2(lb_buf[slot, h * tk:(h + 1) * tk, :] + sfx_buf[:, h * tq:(h + 1) * tq])
            if masked:
                w = jnp.where(valid, w, 0.0)
            w_buf[h * tk:(h + 1) * tk, :] = w.astype(BF16)

    def p4(t):
        vx = vt_ref[:, pl.ds(key_start(t), tk)]
        vbd = jnp.concatenate([vx[:pw], vx[pw:]], axis=1)
        acc[...] += _dot(vbd, w_buf[...])

    def emit(i, active, parity, masked, first):
        if 4 in active:
            p4(i - 4)
        if 3 in active:
            p3(i - 3, (parity - 3) % 2, masked(i - 3))
        if 2 in active:
            p2(first(i - 2))
        if 1 in active:
            p1(i - 1, (parity - 1) % 2, masked(i - 1))
        if 0 in active:
            p0(i)

    def static_iterations(n_tiles, iters):
        for i in iters:
            active = {k for k in range(SB_STAGES) if 0 <= i - k < n_tiles}
            emit(i, active, i % 2, lambda t: t < ndiag, lambda t: t == 0)

    acc[...] = jnp.zeros_like(acc)

    @pl.when(qi == 0)
    def _diagonal_only():
        static_iterations(ndiag, range(ndiag + SB_STAGES - 1))

    @pl.when(qi > 0)
    def _full():
        n = (qi + 1) * ndiag
        head = 2 * ndiag
        static_iterations(head, range(head))
        never = lambda t: False

        def body(jj, _):
            for u in range(2):
                emit(head + 2 * jj + u, set(range(SB_STAGES)), u, never, never)
            return 0

        lax.fori_loop(0, (n - head) // 2, body, 0)
        for e in range(SB_STAGES - 1):
            emit(n + e, {k for k in range(SB_STAGES) if k > e}, e % 2, never, never)

    o_ref[...] = acc[...].T.astype(o_ref.dtype)


def _sb_attention(q, kx, vtx, ut):
    b, t, _ = q.shape
    tq = min(SB_TQ, t)
    tk = SB_TK
    pairs = SB_HEADS // 2
    pw = 2 * SB_HEAD_DIM
    return pl.pallas_call(
        _sb_attn_kernel,
        grid=(b, pairs, t // tq),
        in_specs=[
            pl.BlockSpec((None, tq, pw), lambda bi, pi, qi: (bi, qi, pi)),
            pl.BlockSpec((None, t, 2 * pw), lambda bi, pi, qi: (bi, 0, pi)),
            pl.BlockSpec((None, 2 * pw, t), lambda bi, pi, qi: (bi, pi, 0)),
            pl.BlockSpec(ut.shape, lambda bi, pi, qi: (0, 0)),
        ],
        out_specs=pl.BlockSpec((None, tq, pw), lambda bi, pi, qi: (bi, qi, pi)),
        out_shape=jax.ShapeDtypeStruct((b, t, SB_WIDTH), BF16),
        scratch_shapes=[
            pltpu.VMEM((2 * tk, tq), F32),
            pltpu.VMEM((2, 2 * tk, tq), F32),
            pltpu.VMEM((tk, 2 * tq), BF16),
            pltpu.VMEM((1, 2 * tq), F32),
            pltpu.VMEM((1, 2 * tq), F32),
            pltpu.VMEM((tk, 2 * tq), F32),
            pltpu.VMEM((2 * tk, tq), BF16),
            pltpu.VMEM((pw, tq), F32),
        ],
        compiler_params=_params(("parallel", "parallel", "arbitrary")),
        name="sb_attention",
    )(q, kx, vtx, ut)


def _const_tables():
    i = np.arange(128)
    tril = (i[None, :] <= i[:, None]).astype(np.float32)
    e_mat = np.zeros((128, SSD_D_INNER), np.float32)
    for h in range(SSD_HEADS):
        e_mat[h, h * SSD_HEADDIM:(h + 1) * SSD_HEADDIM] = 1.0
    uo = np.zeros((SB_TK, 2 * SB_TK), np.float32)
    uo[:, :SB_TK] = (i[None, :] > i[:, None])
    uo[:, SB_TK:SB_TK + 3] = 1.0
    hd = np.arange(SB_WIDTH) // SB_HEAD_DIM
    mavg = (hd[:, None] == hd[None, :]).astype(np.float32) / SB_HEAD_DIM
    return (jnp.asarray(tril, BF16), jnp.asarray(e_mat, BF16), jnp.asarray(uo, BF16), jnp.asarray(mavg, BF16))


def _row(v):
    return v.reshape(1, -1).astype(F32)


def _pad_lanes(v, width=128):
    return jnp.pad(v.reshape(1, -1).astype(F32), ((0, 0), (0, width - v.shape[-1])))


def kernel(x, norm_mix, norm_mlp, mlp_w1, mlp_w2, ev_w_in, ev_w_out, ret_qn, ret_kn, ret_gn, ssd_conv_w, ssd_conv_b, ssd_dt_bias, ssd_a_log, ssd_d, ssd_norm, od_w_in, od_w_out, lru_conv_w, lru_conv_b, lru_wa, lru_ba, lru_wx, lru_bx, lru_lam, sb_qn, sb_kn):
    b, t, d = x.shape
    n = b * t
    depth = norm_mix.shape[0]
    tril, e_mat, uo, mavg = _const_tables()
    cos, sin = _rope_tables(t)
    xf = x.reshape(n, d)
    for l in range(depth):
        g_mix = _row(norm_mix[l])
        if l % 2 == 0:
            e = l // 2
            w_in = jnp.pad(ev_w_in[e], ((0, 0), (0, EVEN_IN_PAD - EVEN_IN))).astype(BF16)
            p = _norm_matmul(xf, g_mix, w_in, tn=1920).reshape(b, t, EVEN_IN_PAD)
            y = _even_core(
                p, cos, sin, _row(ret_qn[e]), _row(ret_kn[e]), _row(ret_gn[e]),
                ssd_conv_w[e].astype(F32), _row(ssd_conv_b[e]), _pad_lanes(ssd_dt_bias[e]), _pad_lanes(ssd_a_log[e]),
                _row(jnp.repeat(ssd_d[e], SSD_HEADDIM)), _row(ssd_norm[e]), e_mat, tril)
            xf = _outproj(xf, [y.reshape(n, EVEN_OUT)], [ev_w_out[e].astype(BF16)])
        else:
            o = l // 2
            p = _norm_matmul(xf, g_mix, od_w_in[o].astype(BF16), tn=1792).reshape(b, t, ODD_IN)
            wg = jnp.concatenate([lru_wa[o], lru_wx[o]], axis=-1).astype(BF16)
            yc, q, k, v = _odd_core(
                p, lru_conv_w[o].astype(F32), _row(lru_conv_b[o]), wg, _row(lru_ba[o]), _row(lru_bx[o]),
                _row(lru_lam[o]), _row(jnp.tile(sb_qn[o], SB_HEADS)), _row(jnp.tile(sb_kn[o], SB_HEADS)), mavg)
            yd = _sb_attention(q, k, v, uo)
            w_out = od_w_out[o].astype(BF16)
            xf = _outproj(xf, [yc.reshape(n, LRU_WIDTH), yd.reshape(n, SB_WIDTH)], [w_out[:LRU_WIDTH], w_out[LRU_WIDTH:]])
        xf = _mlp(xf, _row(norm_mlp[l]), mlp_w1[l].astype(BF16), mlp_w2[l].astype(BF16))
    return xf.reshape(b, t, d)
```

```python
import functools
import math

import numpy as np
import jax
import jax.numpy as jnp
from jax import lax
from jax.experimental import pallas as pl
from jax.experimental.pallas import tpu as pltpu

F32 = jnp.float32
BF16 = jnp.bfloat16

D_MODEL = 1024
EPS = 1e-6

RET_HEADS = 4
RET_DK = 128
RET_DV = 256
ROPE_BASE = 10000.0
RET_LOG_GAMMA = tuple(float(np.log1p(-(2.0 ** (-5.0 - h)))) for h in range(RET_HEADS))

SSD_D_INNER = 1024
SSD_HEADDIM = 64
SSD_HEADS = 16
SSD_GROUPS = 2
SSD_STATE = 128
SSD_CONV = 4
SSD_CONV_DIM = SSD_D_INNER + 2 * SSD_GROUPS * SSD_STATE
CHUNK = 128

LRU_WIDTH = 1024
LRU_BLOCKS = 8
LRU_BLOCK = 128
LRU_C = 8.0
LRU_CONV = 4

SB_HEADS = 8
SB_HEAD_DIM = 64
SB_WIDTH = SB_HEADS * SB_HEAD_DIM

D_FF = 4 * D_MODEL

EVEN_IN = 5648
EVEN_IN_PAD = 5760
EVEN_OUT = 2048
ODD_IN = 3584
ODD_OUT = 1536

EV_Q, EV_K, EV_V, EV_G, EV_Z, EV_XBC, EV_DT = 0, 512, 1024, 2048, 3072, 4096, 5632
OD_GATE, OD_XC, OD_Q, OD_K, OD_V = 0, 1024, 2048, 2560, 3072

LOG2E = 1.4426950408889634
SIGN_BIT = np.int32(-2 ** 31)
SUBLANES = 8
CONV_PAD = 8
VMEM_LIMIT = 56 * 1024 * 1024


def _params(sem):
    return pltpu.CompilerParams(dimension_semantics=sem, vmem_limit_bytes=VMEM_LIMIT)


def _dot(a, b):
    return jnp.dot(a, b, preferred_element_type=F32)


def _dot_nt(a, b):
    return lax.dot_general(a, b, (((1,), (1,)), ((), ())), preferred_element_type=F32)


def _dot_tn(a, b):
    return lax.dot_general(a, b, (((0,), (0,)), ((), ())), preferred_element_type=F32)


def _split3(x):
    hi = x.astype(BF16)
    r = x - hi.astype(F32)
    mid = r.astype(BF16)
    lo = (r - mid.astype(F32)).astype(BF16)
    return hi, mid, lo


def _dot_exact_lhs(x, m):
    hi, mid, lo = _split3(x)
    return _dot(hi, m) + _dot(mid, m) + _dot(lo, m)


def _dot_exact_rhs(m, x):
    hi, mid, lo = _split3(x)
    return _dot(m, hi) + _dot(m, mid) + _dot(m, lo)


def _sigmoid(x):
    return 1.0 / (1.0 + jnp.exp(-x))


def _silu(x):
    return x * _sigmoid(x)


def _softplus(x):
    return jnp.maximum(x, 0.0) + jnp.log(1.0 + jnp.exp(-jnp.abs(x)))


def _gelu_tanh(x):
    c = math.sqrt(2.0 / math.pi)
    return x * (0.5 * (1.0 + jnp.tanh(c * (x + 0.044715 * (x * x * x)))))


def _rms(x, g):
    return x * lax.rsqrt(jnp.mean(x * x, axis=-1, keepdims=True) + EPS) * g


def _causal_conv(acc, cw_ref, cb_ref, xpad, tail, reset):
    rows = acc.shape[0]
    taps = cw_ref.shape[0]
    xpad[0:CONV_PAD, :] = jnp.where(reset, 0.0, tail[...])
    xpad[CONV_PAD:CONV_PAD + rows, :] = acc
    y = cb_ref[...]
    for k in range(taps):
        off = CONV_PAD - (taps - 1) + k
        y = y + cw_ref[k:k + 1, :] * xpad[off:off + rows, :]
    tail[...] = xpad[rows:rows + CONV_PAD, :]
    return y


def _norm_proj_kernel(op, conv_width, tiles_per_seq, *refs):
    if conv_width:
        x_ref, g_ref, w_ref, cw_ref, cb_ref, o_ref, h_scr, xpad, tail = refs
    else:
        x_ref, g_ref, w_ref, o_ref, h_scr = refs

    @pl.when(pl.program_id(1) == 0)
    def _():
        h_scr[...] = _rms(x_ref[...], g_ref[...]).astype(BF16)

    acc = _dot(h_scr[...], w_ref[...])
    if op == "none":
        o_ref[...] = acc
    elif op == "silu":
        o_ref[...] = _silu(acc)
    elif op == "gelu":
        o_ref[...] = _gelu_tanh(acc)
    else:
        y = _causal_conv(acc[:, :conv_width], cw_ref, cb_ref, xpad, tail, pl.program_id(0) % tiles_per_seq == 0)
        o_ref[:, :conv_width] = _silu(y) if op == "conv_silu" else y
        if conv_width < acc.shape[1]:
            o_ref[:, conv_width:] = acc[:, conv_width:]


def _norm_proj(x, g, w, tn, op, seq_len, conv=None):
    n, d = x.shape
    n_out = w.shape[1]
    assert n_out % tn == 0 and (conv is None or n_out == tn)
    tm = min(512 if conv else 1024, n)
    in_specs = [
        pl.BlockSpec((tm, d), lambda i, j: (i, 0)),
        pl.BlockSpec((1, d), lambda i, j: (0, 0)),
        pl.BlockSpec((d, tn), lambda i, j: (0, j)),
    ]
    scratch = [pltpu.VMEM((tm, d), BF16)]
    args = [x, g, w]
    conv_width = 0
    if conv:
        cw, cb = conv
        conv_width = cw.shape[1]
        in_specs += [pl.BlockSpec(cw.shape, lambda i, j: (0, 0)), pl.BlockSpec(cb.shape, lambda i, j: (0, 0))]
        scratch += [pltpu.VMEM((tm + CONV_PAD, conv_width), F32), pltpu.VMEM((CONV_PAD, conv_width), F32)]
        args += [cw, cb]
    return pl.pallas_call(
        functools.partial(_norm_proj_kernel, op, conv_width, seq_len // tm),
        grid=(n // tm, n_out // tn),
        in_specs=in_specs,
        out_specs=pl.BlockSpec((tm, tn), lambda i, j: (i, j)),
        out_shape=jax.ShapeDtypeStruct((n, n_out), F32),
        scratch_shapes=scratch,
        compiler_params=_params(("arbitrary", "arbitrary")),
        name="norm_proj",
    )(*args)


def _mlp_kernel(x_ref, g_ref, w1_ref, w2_ref, o_ref, h_scr, acc_scr):
    f = pl.program_id(1)

    @pl.when(f == 0)
    def _():
        h_scr[...] = _rms(x_ref[...], g_ref[...]).astype(BF16)
        acc_scr[...] = jnp.zeros_like(acc_scr)

    a = _dot(h_scr[...], w1_ref[...])
    a = jnp.square(jnp.maximum(a, 0.0)).astype(BF16)
    acc_scr[...] += _dot(a, w2_ref[...])

    @pl.when(f == pl.num_programs(1) - 1)
    def _():
        o_ref[...] = x_ref[...] + acc_scr[...]


def _mlp(x, g, w1, w2):
    n, d = x.shape
    dff = w1.shape[1]
    tm = min(1024, n)
    tf = 512
    return pl.pallas_call(
        _mlp_kernel,
        grid=(n // tm, dff // tf),
        in_specs=[
            pl.BlockSpec((tm, d), lambda i, f: (i, 0)),
            pl.BlockSpec((1, d), lambda i, f: (0, 0)),
            pl.BlockSpec((d, tf), lambda i, f: (0, f)),
            pl.BlockSpec((tf, d), lambda i, f: (f, 0)),
        ],
        out_specs=pl.BlockSpec((tm, d), lambda i, f: (i, 0)),
        out_shape=jax.ShapeDtypeStruct((n, d), F32),
        scratch_shapes=[pltpu.VMEM((tm, d), BF16), pltpu.VMEM((tm, d), F32)],
        compiler_params=_params(("parallel", "arbitrary")),
        name="mlp",
    )(x, g, w1, w2)


def _outproj1_kernel(x_ref, y_ref, w_ref, o_ref):
    o_ref[...] = x_ref[...] + _dot(y_ref[...], w_ref[...])


def _outproj2_kernel(x_ref, ya_ref, yb_ref, wa_ref, wb_ref, o_ref):
    o_ref[...] = x_ref[...] + _dot(ya_ref[...], wa_ref[...]) + _dot(yb_ref[...], wb_ref[...])


def _outproj(x, ys, ws):
    n, d = x.shape
    tm = min(512, n)
    kern = _outproj1_kernel if len(ys) == 1 else _outproj2_kernel
    in_specs = [pl.BlockSpec((tm, d), lambda i: (i, 0))]
    in_specs += [pl.BlockSpec((tm, y.shape[1]), lambda i: (i, 0)) for y in ys]
    in_specs += [pl.BlockSpec(w.shape, lambda i: (0, 0)) for w in ws]
    return pl.pallas_call(
        kern,
        grid=(n // tm,),
        in_specs=in_specs,
        out_specs=pl.BlockSpec((tm, d), lambda i: (i, 0)),
        out_shape=jax.ShapeDtypeStruct((n, d), F32),
        compiler_params=_params(("parallel",)),
        name="outproj",
    )(x, *ys, *ws)


def _rope_kernel(inv_ref, cos_ref, sin_ref):
    rows = cos_ref.shape[0]
    pos = (pl.program_id(0) * rows + lax.broadcasted_iota(jnp.int32, (rows, RET_DK), 0)).astype(F32)
    lane = lax.broadcasted_iota(jnp.int32, (rows, RET_DK), 1)
    ang = pos * inv_ref[...]
    cos_ref[...] = jnp.cos(ang)
    sin_ref[...] = jnp.where(lane < RET_DK // 2, -jnp.sin(ang), jnp.sin(ang))


def _rope_tables(t):
    half = RET_DK // 2
    inv = ROPE_BASE ** (-jnp.arange(half, dtype=F32) / half)
    inv = jnp.concatenate([inv, inv])[None, :]
    rows = min(512, t)
    return pl.pallas_call(
        _rope_kernel,
        grid=(t // rows,),
        in_specs=[pl.BlockSpec((1, RET_DK), lambda i: (0, 0))],
        out_specs=[pl.BlockSpec((rows, RET_DK), lambda i: (i, 0))] * 2,
        out_shape=[jax.ShapeDtypeStruct((t, RET_DK), F32)] * 2,
        compiler_params=_params(("parallel",)),
        name="rope_tables",
    )(inv)


def _even_core_kernel(p_ref, pg_ref, px_ref, cos_ref, sin_ref, qn_ref, kn_ref, gn_ref, dtb_ref, alog_ref,
                      dsk_ref, sg_ref, e_ref, tril_ref, y_ref, rstate, sstate, decay_scr):
    t = pl.program_id(1)
    c = CHUNK
    row = lax.broadcasted_iota(jnp.int32, (c, c), 0)
    col = lax.broadcasted_iota(jnp.int32, (c, c), 1)
    causal = row >= col

    @pl.when(t == 0)
    def _init():
        rstate[...] = jnp.zeros_like(rstate)
        sstate[...] = jnp.zeros_like(sstate)
        rel = (row - col).astype(F32)
        for h in range(RET_HEADS):
            decay_scr[h] = jnp.where(causal, jnp.exp(RET_LOG_GAMMA[h] * jnp.maximum(rel, 0.0)), 0.0)

    cos = cos_ref[...]
    sin = sin_ref[...]
    idx = lax.broadcasted_iota(jnp.int32, (c, 1), 0).astype(F32)
    for h in range(RET_HEADS):
        lg = RET_LOG_GAMMA[h]
        qh = _rms(p_ref[:, EV_Q + h * RET_DK:EV_Q + (h + 1) * RET_DK], qn_ref[...])
        kh = _rms(p_ref[:, EV_K + h * RET_DK:EV_K + (h + 1) * RET_DK], kn_ref[...])
        qh = qh * cos + pltpu.roll(qh, RET_DK // 2, 1) * sin
        kh = (kh * cos + pltpu.roll(kh, RET_DK // 2, 1) * sin) * (RET_DK ** -0.5)
        vb = p_ref[:, EV_V + h * RET_DV:EV_V + (h + 1) * RET_DV].astype(BF16)
        scores = _dot_nt(qh.astype(BF16), kh.astype(BF16)) * decay_scr[h]
        inner = _dot(scores.astype(BF16), vb)
        s_prev = rstate[h]
        q_dec = jnp.exp(lg * (idx + 1.0))
        cross = _dot((qh * q_dec).astype(BF16), s_prev.astype(BF16))
        k_dec = jnp.exp(lg * (float(c - 1) - idx))
        rstate[h] = s_prev * math.exp(lg * c) + _dot_tn((kh * k_dec).astype(BF16), vb)
        ya = inner + cross
        yc = ya - jnp.mean(ya, axis=-1, keepdims=True)
        yn = yc * lax.rsqrt(jnp.mean(yc * yc, axis=-1, keepdims=True) + EPS)
        yn = yn * gn_ref[:, h * RET_DV:(h + 1) * RET_DV]
        gate = pg_ref[:, h * RET_DV:(h + 1) * RET_DV]
        y_ref[:, h * RET_DV:(h + 1) * RET_DV] = (gate * yn).astype(y_ref.dtype)

    xs = px_ref[:, :SSD_D_INNER]
    gs = SSD_GROUPS * SSD_STATE
    bm = px_ref[:, SSD_D_INNER:SSD_D_INNER + gs]
    cm = px_ref[:, SSD_D_INNER + gs:SSD_D_INNER + 2 * gs]

    dt = _softplus(px_ref[:, SSD_CONV_DIM:SSD_CONV_DIM + 128] + dtb_ref[...])
    a_neg = -jnp.exp(alog_ref[...])
    d_a = dt * a_neg
    acum = _dot_exact_rhs(tril_ref[...], d_a)
    acum_t = acum.T
    exp_a = jnp.exp(acum)
    dec = jnp.exp(acum[c - 1:c, :] - acum)
    e_mat = e_ref[...]
    dt_e = _dot_exact_lhs(dt, e_mat)
    exp_a_e = _dot_exact_lhs(exp_a, e_mat)
    dec_e = _dot_exact_lhs(dec, e_mat)
    xr = xs * dt_e
    xrd_b = (xr * dec_e).astype(BF16)
    cdec_e = exp_a_e[c - 1:c, :]
    lane = lax.broadcasted_iota(jnp.int32, (c, 128), 1)
    left = lane < SSD_HEADDIM
    hpg = SSD_HEADS // SSD_GROUPS
    gw = hpg * SSD_HEADDIM
    z_gate = pg_ref[:, EV_Z - EV_G:EV_Z - EV_G + SSD_D_INNER]
    for g in range(SSD_GROUPS):
        bg = bm[:, g * SSD_STATE:(g + 1) * SSD_STATE].astype(BF16)
        cg = cm[:, g * SSD_STATE:(g + 1) * SSD_STATE].astype(BF16)
        cb_mat = _dot_nt(cg, bg)
        s_prev = sstate[g]
        y_off = _dot(cg, s_prev.astype(BF16)) * exp_a_e[:, g * gw:(g + 1) * gw]
        pairs = []
        for pr in range(hpg // 2):
            h0 = g * hpg + 2 * pr
            ms = []
            for h in (h0, h0 + 1):
                seg = acum[:, h:h + 1] - acum_t[h:h + 1, :]
                l_mat = jnp.where(causal, jnp.exp(jnp.minimum(seg, 0.0)), 0.0)
                ms.append((cb_mat * l_mat).astype(BF16))
            xp = xr[:, h0 * SSD_HEADDIM:(h0 + 2) * SSD_HEADDIM]
            xbd = jnp.concatenate([jnp.where(left, xp, 0.0), jnp.where(left, 0.0, xp)], axis=0).astype(BF16)
            pairs.append(_dot(jnp.concatenate(ms, axis=1), xbd))
        y_diag = jnp.concatenate(pairs, axis=1)
        sstate[g] = s_prev * cdec_e[:, g * gw:(g + 1) * gw] + _dot_tn(bg, xrd_b[:, g * gw:(g + 1) * gw])
        yb = y_diag + y_off + xs[:, g * gw:(g + 1) * gw] * dsk_ref[:, g * gw:(g + 1) * gw]
        yb = yb * z_gate[:, g * gw:(g + 1) * gw]
        yb = _rms(yb, sg_ref[:, g * gw:(g + 1) * gw])
        lo = RET_HEADS * RET_DV + g * gw
        y_ref[:, lo:lo + gw] = yb.astype(y_ref.dtype)


def _even_core(p, pg, px, cos, sin, qn, kn, gn, dtb, alog, dsk, sg, e_mat, tril):
    b, t, _ = p.shape
    c = CHUNK
    full = lambda a: pl.BlockSpec(a.shape, lambda bi, ti: (0,) * a.ndim)
    smalls = (qn, kn, gn, dtb, alog, dsk, sg, e_mat, tril)
    return pl.pallas_call(
        _even_core_kernel,
        grid=(b, t // c),
        in_specs=[
            pl.BlockSpec((None, c, EV_G), lambda bi, ti: (bi, ti, 0)),
            pl.BlockSpec((None, c, EV_XBC - EV_G), lambda bi, ti: (bi, ti, 0)),
            pl.BlockSpec((None, c, EVEN_IN_PAD - EV_XBC), lambda bi, ti: (bi, ti, 0)),
            pl.BlockSpec((c, RET_DK), lambda bi, ti: (ti, 0)),
            pl.BlockSpec((c, RET_DK), lambda bi, ti: (ti, 0)),
        ] + [full(a) for a in smalls],
        out_specs=pl.BlockSpec((None, c, EVEN_OUT), lambda bi, ti: (bi, ti, 0)),
        out_shape=jax.ShapeDtypeStruct((b, t, EVEN_OUT), BF16),
        scratch_shapes=[
            pltpu.VMEM((RET_HEADS, RET_DK, RET_DV), F32),
            pltpu.VMEM((SSD_GROUPS, SSD_STATE, SSD_D_INNER // SSD_GROUPS), F32),
            pltpu.VMEM((RET_HEADS, c, c), F32),
        ],
        compiler_params=_params(("parallel", "arbitrary")),
        name="even_core",
    )(p, pg, px, cos, sin, *smalls)


def _odd_core_kernel(pg_ref, pc_ref, pq_ref, wg_ref, ba_ref, bx_ref, lam_ref, qn_ref, kn_ref, mavg_ref,
                     yc_ref, q_ref, k_ref, v_ref, a_scr, b_scr, hcarry):
    t = pl.program_id(1)
    r = pc_ref.shape[0]

    @pl.when(t == 0)
    def _init():
        hcarry[...] = jnp.zeros_like(hcarry)

    xc = pc_ref[...]

    rs, is_ = [], []
    for blk in range(LRU_BLOCKS):
        g = _dot(xc[:, blk * LRU_BLOCK:(blk + 1) * LRU_BLOCK].astype(BF16), wg_ref[blk])
        rs.append(g[:, :LRU_BLOCK])
        is_.append(g[:, LRU_BLOCK:])
    rg = _sigmoid(jnp.concatenate(rs, axis=1) + ba_ref[...])
    ig = _sigmoid(jnp.concatenate(is_, axis=1) + bx_ref[...])
    log_a = -LRU_C * rg * _softplus(-lam_ref[...])
    a_scr[...] = jnp.exp(log_a)
    b_scr[...] = jnp.sqrt(1.0 - jnp.exp(2.0 * log_a)) * (ig * xc)

    rowi = lax.broadcasted_iota(jnp.int32, (SUBLANES, LRU_WIDTH), 0)

    def body(gi, carry):
        off = pl.multiple_of(gi * SUBLANES, SUBLANES)
        a = a_scr[pl.ds(off, SUBLANES), :]
        bv = b_scr[pl.ds(off, SUBLANES), :]
        for s in (1, 2, 4):
            a_sh = jnp.where(rowi >= s, pltpu.roll(a, s, 0), 1.0)
            b_sh = jnp.where(rowi >= s, pltpu.roll(bv, s, 0), 0.0)
            bv = a * b_sh + bv
            a = a * a_sh
        h = a * carry + bv
        b_scr[pl.ds(off, SUBLANES), :] = h
        return jnp.broadcast_to(h[SUBLANES - 1:SUBLANES, :], (SUBLANES, LRU_WIDTH))

    hcarry[...] = lax.fori_loop(0, r // SUBLANES, body, hcarry[...])
    yc_ref[...] = (b_scr[...] * pg_ref[...]).astype(yc_ref.dtype)

    def headnorm(x, g):
        sq = x * x
        hi = sq.astype(BF16)
        lo = (sq - hi.astype(F32)).astype(BF16)
        ms = _dot(hi, mavg_ref[...]) + _dot(lo, mavg_ref[...])
        return x * lax.rsqrt(ms + EPS) * g

    q = headnorm(pq_ref[:, 0:SB_WIDTH], qn_ref[...]) * (SB_HEAD_DIM ** -0.5 * LOG2E)
    q_ref[...] = q.astype(q_ref.dtype)
    kn = headnorm(pq_ref[:, SB_WIDTH:2 * SB_WIDTH], kn_ref[...])
    vt = pq_ref[:, 2 * SB_WIDTH:3 * SB_WIDTH].T
    pw = 2 * SB_HEAD_DIM
    lane = lax.broadcasted_iota(jnp.int32, (r, pw), 1)
    sub = lax.broadcasted_iota(jnp.int32, (pw, r), 0)
    for pr in range(SB_HEADS // 2):
        kp = kn[:, pr * pw:(pr + 1) * pw]
        k_ref[:, 2 * pr * pw:(2 * pr + 1) * pw] = jnp.where(lane < SB_HEAD_DIM, kp, 0.0).astype(k_ref.dtype)
        k_ref[:, (2 * pr + 1) * pw:(2 * pr + 2) * pw] = jnp.where(lane < SB_HEAD_DIM, 0.0, kp).astype(k_ref.dtype)
        vp = vt[pr * pw:(pr + 1) * pw, :]
        v_ref[2 * pr * pw:(2 * pr + 1) * pw, :] = jnp.where(sub < SB_HEAD_DIM, vp, 0.0).astype(v_ref.dtype)
        v_ref[(2 * pr + 1) * pw:(2 * pr + 2) * pw, :] = jnp.where(sub < SB_HEAD_DIM, 0.0, vp).astype(v_ref.dtype)


def _odd_core(pg, pc, pq, wg, ba, bx, lam, qn, kn, mavg):
    b, t, _ = pg.shape
    r = min(256, t)
    full = lambda a: pl.BlockSpec(a.shape, lambda bi, ti: (0,) * a.ndim)
    smalls = (wg, ba, bx, lam, qn, kn, mavg)
    blk = lambda w: pl.BlockSpec((None, r, w), lambda bi, ti: (bi, ti, 0))
    return pl.pallas_call(
        _odd_core_kernel,
        grid=(b, t // r),
        in_specs=[blk(LRU_WIDTH), blk(LRU_WIDTH), blk(ODD_IN - OD_Q)] + [full(a) for a in smalls],
        out_specs=[blk(LRU_WIDTH), blk(SB_WIDTH), blk(2 * SB_WIDTH),
                   pl.BlockSpec((None, 2 * SB_WIDTH, r), lambda bi, ti: (bi, 0, ti))],
        out_shape=[jax.ShapeDtypeStruct((b, t, LRU_WIDTH), BF16), jax.ShapeDtypeStruct((b, t, SB_WIDTH), BF16),
                   jax.ShapeDtypeStruct((b, t, 2 * SB_WIDTH), BF16), jax.ShapeDtypeStruct((b, 2 * SB_WIDTH, t), BF16)],
        scratch_shapes=[
            pltpu.VMEM((r, LRU_WIDTH), F32),
            pltpu.VMEM((r, LRU_WIDTH), F32),
            pltpu.VMEM((SUBLANES, LRU_WIDTH), F32),
        ],
        compiler_params=_params(("parallel", "arbitrary")),
        name="odd_core",
    )(pg, pc, pq, *smalls)


SB_TQ = 512
SB_TK = 128
SB_UNROLL = 2
SB_STAGES = 5


def _sb_attn_kernel(q_ref, k_ref, vt_ref, ut_ref, o_ref, z_buf, lb_buf, l1b_buf, r0_buf, r0_prev, c_buf, sfx_buf, w_buf, acc):
    qi = pl.program_id(2)
    tq = q_ref.shape[0]
    tk = SB_TK
    pw = 2 * SB_HEAD_DIM
    ndiag = tq // tk
    jmax = qi * ndiag + (ndiag - 1)
    kloc = lax.broadcasted_iota(jnp.int32, (tk, tq), 0)
    qpos = qi * tq + lax.broadcasted_iota(jnp.int32, (tk, tq), 1)

    def key_start(t):
        return pl.multiple_of((jmax - t) * tk, tk)

    def p0(t):
        kx = k_ref[pl.ds(key_start(t), tk), :]
        km = jnp.concatenate([kx[:, :pw], kx[:, pw:]], axis=0)
        z_buf[...] = _dot_nt(km, q_ref[...])

    def p1(t, slot, masked):
        if masked:
            valid = (key_start(t) + kloc) < qpos
        for h in range(2):
            z = z_buf[h * tk:(h + 1) * tk, :]
            neg_abs = lax.bitcast_convert_type(lax.bitcast_convert_type(z, jnp.int32) | SIGN_BIT, F32)
            sp = jnp.log(1.0 + jnp.exp2(neg_abs)) * LOG2E
            lb = jnp.minimum(z, 0.0) - sp
            l1 = lb - z
            if masked:
                l1 = jnp.where(valid, l1, 0.0)
            lb_buf[slot, h * tk:(h + 1) * tk, :] = lb
            l1b_buf[:, h * tq:(h + 1) * tq] = l1.astype(BF16)
            r0_buf[:, h * tq:(h + 1) * tq] = l1[0:1, :]

    def p2():
        sfx_buf[...] = _dot(ut_ref[...], l1b_buf[...])
        r0_prev[...] = r0_buf[...]

    def p3(t, slot, masked):
        if masked:
            valid = (key_start(t) + kloc) < qpos
        carry = c_buf[...]
        for h in range(2):
            sfx = sfx_buf[:, h * tq:(h + 1) * tq] + carry[:, h * tq:(h + 1) * tq]
            w = jnp.exp2(lb_buf[slot, h * tk:(h + 1) * tk, :] + sfx)
            if masked:
                w = jnp.where(valid, w, 0.0)
            w_buf[h * tk:(h + 1) * tk, :] = w.astype(BF16)
        c_buf[...] = carry + sfx_buf[0:1, :] + r0_prev[...]

    def p4(t):
        vx = vt_ref[:, pl.ds(key_start(t), tk)]
        vbd = jnp.concatenate([vx[:pw], vx[pw:]], axis=1)
        acc[...] += _dot(vbd, w_buf[...])

    def emit(i, active, parity, masked):
        if 4 in active:
            p4(i - 4)
        if 3 in active:
            p3(i - 3, (parity - 3) % 2, masked(i - 3))
        if 2 in active:
            p2()
        if 1 in active:
            p1(i - 1, (parity - 1) % 2, masked(i - 1))
        if 0 in active:
            p0(i)

    def static_iterations(n_tiles, iters):
        for i in iters:
            active = {k for k in range(SB_STAGES) if 0 <= i - k < n_tiles}
            emit(i, active, i % 2, lambda t: t < ndiag)

    acc[...] = jnp.zeros_like(acc)
    c_buf[...] = jnp.zeros_like(c_buf)

    @pl.when(qi == 0)
    def _diagonal_only():
        static_iterations(ndiag, range(ndiag + SB_STAGES - 1))

    @pl.when(qi > 0)
    def _full():
        n = (qi + 1) * ndiag
        head = 2 * ndiag
        static_iterations(head, range(head))
        never = lambda t: False

        def body(jj, _):
            for u in range(SB_UNROLL):
                emit(head + SB_UNROLL * jj + u, set(range(SB_STAGES)), u % 2, never)
            return 0

        lax.fori_loop(0, (n - head) // SB_UNROLL, body, 0)
        for e in range(SB_STAGES - 1):
            emit(n + e, {k for k in range(SB_STAGES) if k > e}, e % 2, never)

    o_ref[...] = acc[...].T.astype(o_ref.dtype)


def _sb_attention(q, kx, vtx, ut):
    b, t, _ = q.shape
    tq = min(SB_TQ, t)
    tk = SB_TK
    pairs = SB_HEADS // 2
    pw = 2 * SB_HEAD_DIM
    return pl.pallas_call(
        _sb_attn_kernel,
        grid=(b, pairs, t // tq),
        in_specs=[
            pl.BlockSpec((None, tq, pw), lambda bi, pi, qi: (bi, qi, pi)),
            pl.BlockSpec((None, t, 2 * pw), lambda bi, pi, qi: (bi, 0, pi)),
            pl.BlockSpec((None, 2 * pw, t), lambda bi, pi, qi: (bi, pi, 0)),
            pl.BlockSpec(ut.shape, lambda bi, pi, qi: (0, 0)),
        ],
        out_specs=pl.BlockSpec((None, tq, pw), lambda bi, pi, qi: (bi, qi, pi)),
        out_shape=jax.ShapeDtypeStruct((b, t, SB_WIDTH), BF16),
        scratch_shapes=[
            pltpu.VMEM((2 * tk, tq), F32),
            pltpu.VMEM((2, 2 * tk, tq), F32),
            pltpu.VMEM((tk, 2 * tq), BF16),
            pltpu.VMEM((1, 2 * tq), F32),
            pltpu.VMEM((1, 2 * tq), F32),
            pltpu.VMEM((1, 2 * tq), F32),
            pltpu.VMEM((tk, 2 * tq), F32),
            pltpu.VMEM((2 * tk, tq), BF16),
            pltpu.VMEM((pw, tq), F32),
        ],
        compiler_params=_params(("parallel", "parallel", "arbitrary")),
        name="sb_attention",
    )(q, kx, vtx, ut)


def _const_tables():
    i = np.arange(128)
    tril = (i[None, :] <= i[:, None]).astype(np.float32)
    e_mat = np.zeros((128, SSD_D_INNER), np.float32)
    for h in range(SSD_HEADS):
        e_mat[h, h * SSD_HEADDIM:(h + 1) * SSD_HEADDIM] = 1.0
    uo = (i[None, :] > i[:, None]).astype(np.float32)
    hd = np.arange(SB_WIDTH) // SB_HEAD_DIM
    mavg = (hd[:, None] == hd[None, :]).astype(np.float32) / SB_HEAD_DIM
    return (jnp.asarray(tril, BF16), jnp.asarray(e_mat, BF16), jnp.asarray(uo, BF16), jnp.asarray(mavg, BF16))


def _row(v):
    return v.reshape(1, -1).astype(F32)


def _pad_lanes(v, width=128):
    return jnp.pad(v.reshape(1, -1).astype(F32), ((0, 0), (0, width - v.shape[-1])))


def kernel(x, norm_mix, norm_mlp, mlp_w1, mlp_w2, ev_w_in, ev_w_out, ret_qn, ret_kn, ret_gn, ssd_conv_w, ssd_conv_b, ssd_dt_bias, ssd_a_log, ssd_d, ssd_norm, od_w_in, od_w_out, lru_conv_w, lru_conv_b, lru_wa, lru_ba, lru_wx, lru_bx, lru_lam, sb_qn, sb_kn):
    b, t, d = x.shape
    n = b * t
    depth = norm_mix.shape[0]
    tril, e_mat, uo, mavg = _const_tables()
    cos, sin = _rope_tables(t)
    xf = x.reshape(n, d)
    for l in range(depth):
        g_mix = _row(norm_mix[l])
        if l % 2 == 0:
            e = l // 2
            w_in = jnp.pad(ev_w_in[e], ((0, 0), (0, EVEN_IN_PAD - EVEN_IN))).astype(BF16)
            p = _norm_proj(xf, g_mix, w_in[:, :EV_G], 1024, "none", t)
            pg = _norm_proj(xf, g_mix, w_in[:, EV_G:EV_XBC], 1024, "silu", t)
            px = _norm_proj(xf, g_mix, w_in[:, EV_XBC:], EVEN_IN_PAD - EV_XBC, "conv_silu", t,
                            conv=(ssd_conv_w[e].astype(F32), _row(ssd_conv_b[e])))
            y = _even_core(
                p.reshape(b, t, EV_G), pg.reshape(b, t, EV_XBC - EV_G), px.reshape(b, t, EVEN_IN_PAD - EV_XBC), cos, sin,
                _row(ret_qn[e]), _row(ret_kn[e]), _row(ret_gn[e]), _pad_lanes(ssd_dt_bias[e]), _pad_lanes(ssd_a_log[e]),
                _row(jnp.repeat(ssd_d[e], SSD_HEADDIM)), _row(ssd_norm[e]), e_mat, tril)
            xf = _outproj(xf, [y.reshape(n, EVEN_OUT)], [ev_w_out[e].astype(BF16)])
        else:
            o = l // 2
            w_in = od_w_in[o].astype(BF16)
            pg = _norm_proj(xf, g_mix, w_in[:, :OD_XC], LRU_WIDTH, "gelu", t)
            pc = _norm_proj(xf, g_mix, w_in[:, OD_XC:OD_Q], LRU_WIDTH, "conv", t,
                            conv=(lru_conv_w[o].astype(F32), _row(lru_conv_b[o])))
            pq = _norm_proj(xf, g_mix, w_in[:, OD_Q:], ODD_IN - OD_Q, "none", t)
            wg = jnp.concatenate([lru_wa[o], lru_wx[o]], axis=-1).astype(BF16)
            yc, q, k, v = _odd_core(
                pg.reshape(b, t, LRU_WIDTH), pc.reshape(b, t, LRU_WIDTH), pq.reshape(b, t, ODD_IN - OD_Q),
                wg, _row(lru_ba[o]), _row(lru_bx[o]),
                _row(lru_lam[o]), _row(jnp.tile(sb_qn[o], SB_HEADS)), _row(jnp.tile(sb_kn[o], SB_HEADS)), mavg)
            yd = _sb_attention(q, k, v, uo)
            w_out = od_w_out[o].astype(BF16)
            xf = _outproj(xf, [yc.reshape(n, LRU_WIDTH), yd.reshape(n, SB_WIDTH)], [w_out[:LRU_WIDTH], w_out[LRU_WIDTH:]])
        xf = _mlp(xf, _row(norm_mlp[l]), mlp_w1[l].astype(BF16), mlp_w2[l].astype(BF16))
    return xf.reshape(b, t, d)
```

```python
import functools
import math

import numpy as np
import jax
import jax.numpy as jnp
from jax import lax
from jax.experimental import pallas as pl
from jax.experimental.pallas import tpu as pltpu

F32 = jnp.float32
BF16 = jnp.bfloat16

D_MODEL = 1024
EPS = 1e-6

RET_HEADS = 4
RET_DK = 128
RET_DV = 256
ROPE_BASE = 10000.0
RET_LOG_GAMMA = tuple(float(np.log1p(-(2.0 ** (-5.0 - h)))) for h in range(RET_HEADS))

SSD_D_INNER = 1024
SSD_HEADDIM = 64
SSD_HEADS = 16
SSD_GROUPS = 2
SSD_STATE = 128
SSD_CONV = 4
SSD_CONV_DIM = SSD_D_INNER + 2 * SSD_GROUPS * SSD_STATE
CHUNK = 128

LRU_WIDTH = 1024
LRU_BLOCKS = 8
LRU_BLOCK = 128
LRU_C = 8.0
LRU_CONV = 4

SB_HEADS = 8
SB_HEAD_DIM = 64
SB_WIDTH = SB_HEADS * SB_HEAD_DIM

D_FF = 4 * D_MODEL

EVEN_IN = 5648
EVEN_IN_PAD = 5760
EVEN_OUT = 2048
ODD_IN = 3584
ODD_OUT = 1536

EV_Q, EV_K, EV_V, EV_G, EV_Z, EV_XBC, EV_DT = 0, 512, 1024, 2048, 3072, 4096, 5632
OD_GATE, OD_XC, OD_Q, OD_K, OD_V = 0, 1024, 2048, 2560, 3072

LOG2E = 1.4426950408889634
SIGN_BIT = np.int32(-2 ** 31)
SUBLANES = 8
CONV_PAD = 8
VMEM_LIMIT = 56 * 1024 * 1024


def _params(sem):
    return pltpu.CompilerParams(dimension_semantics=sem, vmem_limit_bytes=VMEM_LIMIT)


def _dot(a, b):
    return jnp.dot(a, b, preferred_element_type=F32)


def _dot_nt(a, b):
    return lax.dot_general(a, b, (((1,), (1,)), ((), ())), preferred_element_type=F32)


def _dot_tn(a, b):
    return lax.dot_general(a, b, (((0,), (0,)), ((), ())), preferred_element_type=F32)


def _split3(x):
    hi = x.astype(BF16)
    r = x - hi.astype(F32)
    mid = r.astype(BF16)
    lo = (r - mid.astype(F32)).astype(BF16)
    return hi, mid, lo


def _dot_exact_lhs(x, m):
    hi, mid, lo = _split3(x)
    return _dot(hi, m) + _dot(mid, m) + _dot(lo, m)


def _dot_exact_rhs(m, x):
    hi, mid, lo = _split3(x)
    return _dot(m, hi) + _dot(m, mid) + _dot(m, lo)


def _sigmoid(x):
    return 1.0 / (1.0 + jnp.exp(-x))


def _silu(x):
    return x * _sigmoid(x)


def _softplus(x):
    return jnp.maximum(x, 0.0) + jnp.log(1.0 + jnp.exp(-jnp.abs(x)))


def _gelu_tanh(x):
    c = math.sqrt(2.0 / math.pi)
    return x * (0.5 * (1.0 + jnp.tanh(c * (x + 0.044715 * (x * x * x)))))


def _rms(x, g):
    return x * lax.rsqrt(jnp.mean(x * x, axis=-1, keepdims=True) + EPS) * g


def _causal_conv(acc, cw_ref, cb_ref, xpad, tail, reset):
    rows = acc.shape[0]
    taps = cw_ref.shape[0]
    xpad[0:CONV_PAD, :] = jnp.where(reset, 0.0, tail[...])
    xpad[CONV_PAD:CONV_PAD + rows, :] = acc
    y = cb_ref[...]
    for k in range(taps):
        off = CONV_PAD - (taps - 1) + k
        y = y + cw_ref[k:k + 1, :] * xpad[off:off + rows, :]
    tail[...] = xpad[rows:rows + CONV_PAD, :]
    return y


_ACTS = {"silu": _silu, "gelu": _gelu_tanh}


def _norm_proj_act_kernel(act, act_from, x_ref, g_ref, w_ref, o_ref, h_scr):
    j = pl.program_id(1)

    @pl.when(j == 0)
    def _():
        h_scr[...] = _rms(x_ref[...], g_ref[...]).astype(BF16)

    acc = _dot(h_scr[...], w_ref[...])
    o_ref[...] = jnp.where(j >= act_from, _ACTS[act](acc), acc).astype(o_ref.dtype)


def _norm_proj_act(x, g, w, tn, act, act_from):
    n, d = x.shape
    n_out = w.shape[1]
    tm = min(1024, n)
    return pl.pallas_call(
        functools.partial(_norm_proj_act_kernel, act, act_from),
        grid=(n // tm, n_out // tn),
        in_specs=[
            pl.BlockSpec((tm, d), lambda i, j: (i, 0)),
            pl.BlockSpec((1, d), lambda i, j: (0, 0)),
            pl.BlockSpec((d, tn), lambda i, j: (0, j)),
        ],
        out_specs=pl.BlockSpec((tm, tn), lambda i, j: (i, j)),
        out_shape=jax.ShapeDtypeStruct((n, n_out), BF16),
        scratch_shapes=[pltpu.VMEM((tm, d), BF16)],
        compiler_params=_params(("parallel", "arbitrary")),
        name="norm_proj_act",
    )(x, g, w)


def _norm_proj_conv_kernel(silu, tiles_per_seq, x_ref, g_ref, w_ref, cw_ref, cb_ref, *rest):
    cwidth = cw_ref.shape[1]
    if w_ref.shape[1] > cwidth:
        o_ref, raw_ref, xpad, tail = rest
    else:
        o_ref, xpad, tail = rest
    h = _rms(x_ref[...], g_ref[...]).astype(BF16)
    acc = _dot(h, w_ref[...])
    y = _causal_conv(acc[:, :cwidth], cw_ref, cb_ref, xpad, tail, pl.program_id(0) % tiles_per_seq == 0)
    o_ref[...] = (_silu(y) if silu else y).astype(o_ref.dtype)
    if w_ref.shape[1] > cwidth:
        raw_ref[...] = acc[:, cwidth:]


def _norm_proj_conv(x, g, w, cw, cb, silu, seq_len):
    n, d = x.shape
    n_out = w.shape[1]
    cwidth = cw.shape[1]
    tm = min(512, n)
    row = lambda width: pl.BlockSpec((tm, width), lambda i: (i, 0))
    full = lambda a: pl.BlockSpec(a.shape, lambda i: (0, 0))
    out_specs = [row(cwidth)]
    out_shape = [jax.ShapeDtypeStruct((n, cwidth), BF16)]
    if n_out > cwidth:
        out_specs.append(row(n_out - cwidth))
        out_shape.append(jax.ShapeDtypeStruct((n, n_out - cwidth), F32))
    return pl.pallas_call(
        functools.partial(_norm_proj_conv_kernel, silu, seq_len // tm),
        grid=(n // tm,),
        in_specs=[row(d), full(g), full(w), full(cw), full(cb)],
        out_specs=out_specs,
        out_shape=out_shape,
        scratch_shapes=[pltpu.VMEM((tm + CONV_PAD, cwidth), F32), pltpu.VMEM((CONV_PAD, cwidth), F32)],
        compiler_params=_params(("arbitrary",)),
        name="norm_proj_conv",
    )(x, g, w, cw, cb)


def _mlp_kernel(x_ref, g_ref, w1_ref, w2_ref, o_ref, h_scr, acc_scr):
    f = pl.program_id(1)

    @pl.when(f == 0)
    def _():
        h_scr[...] = _rms(x_ref[...], g_ref[...]).astype(BF16)
        acc_scr[...] = jnp.zeros_like(acc_scr)

    a = _dot(h_scr[...], w1_ref[...])
    a = jnp.square(jnp.maximum(a, 0.0)).astype(BF16)
    acc_scr[...] += _dot(a, w2_ref[...])

    @pl.when(f == pl.num_programs(1) - 1)
    def _():
        o_ref[...] = x_ref[...] + acc_scr[...]


def _mlp(x, g, w1, w2):
    n, d = x.shape
    dff = w1.shape[1]
    tm = min(1024, n)
    tf = 512
    return pl.pallas_call(
        _mlp_kernel,
        grid=(n // tm, dff // tf),
        in_specs=[
            pl.BlockSpec((tm, d), lambda i, f: (i, 0)),
            pl.BlockSpec((1, d), lambda i, f: (0, 0)),
            pl.BlockSpec((d, tf), lambda i, f: (0, f)),
            pl.BlockSpec((tf, d), lambda i, f: (f, 0)),
        ],
        out_specs=pl.BlockSpec((tm, d), lambda i, f: (i, 0)),
        out_shape=jax.ShapeDtypeStruct((n, d), F32),
        scratch_shapes=[pltpu.VMEM((tm, d), BF16), pltpu.VMEM((tm, d), F32)],
        compiler_params=_params(("parallel", "arbitrary")),
        name="mlp",
    )(x, g, w1, w2)


def _outproj1_kernel(x_ref, y_ref, w_ref, o_ref):
    o_ref[...] = x_ref[...] + _dot(y_ref[...], w_ref[...])


def _outproj2_kernel(x_ref, ya_ref, yb_ref, wa_ref, wb_ref, o_ref):
    o_ref[...] = x_ref[...] + _dot(ya_ref[...], wa_ref[...]) + _dot(yb_ref[...], wb_ref[...])


def _outproj(x, ys, ws):
    n, d = x.shape
    tm = min(512, n)
    kern = _outproj1_kernel if len(ys) == 1 else _outproj2_kernel
    in_specs = [pl.BlockSpec((tm, d), lambda i: (i, 0))]
    in_specs += [pl.BlockSpec((tm, y.shape[1]), lambda i: (i, 0)) for y in ys]
    in_specs += [pl.BlockSpec(w.shape, lambda i: (0, 0)) for w in ws]
    return pl.pallas_call(
        kern,
        grid=(n // tm,),
        in_specs=in_specs,
        out_specs=pl.BlockSpec((tm, d), lambda i: (i, 0)),
        out_shape=jax.ShapeDtypeStruct((n, d), F32),
        compiler_params=_params(("parallel",)),
        name="outproj",
    )(x, *ys, *ws)


def _rope_kernel(inv_ref, cos_ref, sin_ref):
    rows = cos_ref.shape[0]
    pos = (pl.program_id(0) * rows + lax.broadcasted_iota(jnp.int32, (rows, RET_DK), 0)).astype(F32)
    lane = lax.broadcasted_iota(jnp.int32, (rows, RET_DK), 1)
    ang = pos * inv_ref[...]
    cos_ref[...] = jnp.cos(ang)
    sin_ref[...] = jnp.where(lane < RET_DK // 2, -jnp.sin(ang), jnp.sin(ang))


def _rope_tables(t):
    half = RET_DK // 2
    inv = ROPE_BASE ** (-jnp.arange(half, dtype=F32) / half)
    inv = jnp.concatenate([inv, inv])[None, :]
    rows = min(512, t)
    return pl.pallas_call(
        _rope_kernel,
        grid=(t // rows,),
        in_specs=[pl.BlockSpec((1, RET_DK), lambda i: (0, 0))],
        out_specs=[pl.BlockSpec((rows, RET_DK), lambda i: (i, 0))] * 2,
        out_shape=[jax.ShapeDtypeStruct((t, RET_DK), F32)] * 2,
        compiler_params=_params(("parallel",)),
        name="rope_tables",
    )(inv)


def _even_core_kernel(p_ref, px_ref, dt_ref, cos_ref, sin_ref, qn_ref, kn_ref, gn_ref, dtb_ref, alog_ref,
                      dsk_ref, sg_ref, e_ref, tril_ref, y_ref, rstate, sstate, decay_scr):
    t = pl.program_id(1)
    c = CHUNK
    row = lax.broadcasted_iota(jnp.int32, (c, c), 0)
    col = lax.broadcasted_iota(jnp.int32, (c, c), 1)
    causal = row >= col

    @pl.when(t == 0)
    def _init():
        rstate[...] = jnp.zeros_like(rstate)
        sstate[...] = jnp.zeros_like(sstate)
        rel = (row - col).astype(F32)
        for h in range(RET_HEADS):
            decay_scr[h] = jnp.where(causal, jnp.exp(RET_LOG_GAMMA[h] * jnp.maximum(rel, 0.0)), 0.0)

    cos = cos_ref[...]
    sin = sin_ref[...]
    idx = lax.broadcasted_iota(jnp.int32, (c, 1), 0).astype(F32)
    for h in range(RET_HEADS):
        lg = RET_LOG_GAMMA[h]
        qh = _rms(p_ref[:, EV_Q + h * RET_DK:EV_Q + (h + 1) * RET_DK].astype(F32), qn_ref[...])
        kh = _rms(p_ref[:, EV_K + h * RET_DK:EV_K + (h + 1) * RET_DK].astype(F32), kn_ref[...])
        qh = qh * cos + pltpu.roll(qh, RET_DK // 2, 1) * sin
        kh = (kh * cos + pltpu.roll(kh, RET_DK // 2, 1) * sin) * (RET_DK ** -0.5)
        vb = p_ref[:, EV_V + h * RET_DV:EV_V + (h + 1) * RET_DV]
        scores = _dot_nt(qh.astype(BF16), kh.astype(BF16)) * decay_scr[h]
        inner = _dot(scores.astype(BF16), vb)
        s_prev = rstate[h]
        q_dec = jnp.exp(lg * (idx + 1.0))
        cross = _dot((qh * q_dec).astype(BF16), s_prev.astype(BF16))
        k_dec = jnp.exp(lg * (float(c - 1) - idx))
        rstate[h] = s_prev * math.exp(lg * c) + _dot_tn((kh * k_dec).astype(BF16), vb)
        ya = inner + cross
        yc = ya - jnp.mean(ya, axis=-1, keepdims=True)
        yn = yc * lax.rsqrt(jnp.mean(yc * yc, axis=-1, keepdims=True) + EPS)
        yn = yn * gn_ref[:, h * RET_DV:(h + 1) * RET_DV]
        gate = p_ref[:, EV_G + h * RET_DV:EV_G + (h + 1) * RET_DV].astype(F32)
        y_ref[:, h * RET_DV:(h + 1) * RET_DV] = (gate * yn).astype(y_ref.dtype)

    xs = px_ref[:, :SSD_D_INNER].astype(F32)
    gs = SSD_GROUPS * SSD_STATE
    bm = px_ref[:, SSD_D_INNER:SSD_D_INNER + gs]
    cm = px_ref[:, SSD_D_INNER + gs:SSD_D_INNER + 2 * gs]

    dt = _softplus(dt_ref[...] + dtb_ref[...])
    a_neg = -jnp.exp(alog_ref[...])
    d_a = dt * a_neg
    acum = _dot_exact_rhs(tril_ref[...], d_a)
    acum_t = acum.T
    exp_a = jnp.exp(acum)
    dec = jnp.exp(acum[c - 1:c, :] - acum)
    e_mat = e_ref[...]
    dt_e = _dot_exact_lhs(dt, e_mat)
    exp_a_e = _dot_exact_lhs(exp_a, e_mat)
    dec_e = _dot_exact_lhs(dec, e_mat)
    xr = xs * dt_e
    xrd_b = (xr * dec_e).astype(BF16)
    cdec_e = exp_a_e[c - 1:c, :]
    lane = lax.broadcasted_iota(jnp.int32, (c, 128), 1)
    left = lane < SSD_HEADDIM
    hpg = SSD_HEADS // SSD_GROUPS
    gw = hpg * SSD_HEADDIM
    z_gate = p_ref[:, EV_Z:EV_Z + SSD_D_INNER].astype(F32)
    for g in range(SSD_GROUPS):
        bg = bm[:, g * SSD_STATE:(g + 1) * SSD_STATE]
        cg = cm[:, g * SSD_STATE:(g + 1) * SSD_STATE]
        cb_mat = _dot_nt(cg, bg)
        s_prev = sstate[g]
        y_off = _dot(cg, s_prev.astype(BF16)) * exp_a_e[:, g * gw:(g + 1) * gw]
        pairs = []
        for pr in range(hpg // 2):
            h0 = g * hpg + 2 * pr
            ms = []
            for h in (h0, h0 + 1):
                seg = acum[:, h:h + 1] - acum_t[h:h + 1, :]
                l_mat = jnp.where(causal, jnp.exp(jnp.minimum(seg, 0.0)), 0.0)
                ms.append((cb_mat * l_mat).astype(BF16))
            xp = xr[:, h0 * SSD_HEADDIM:(h0 + 2) * SSD_HEADDIM]
            xbd = jnp.concatenate([jnp.where(left, xp, 0.0), jnp.where(left, 0.0, xp)], axis=0).astype(BF16)
            pairs.append(_dot(jnp.concatenate(ms, axis=1), xbd))
        y_diag = jnp.concatenate(pairs, axis=1)
        sstate[g] = s_prev * cdec_e[:, g * gw:(g + 1) * gw] + _dot_tn(bg, xrd_b[:, g * gw:(g + 1) * gw])
        yb = y_diag + y_off + xs[:, g * gw:(g + 1) * gw] * dsk_ref[:, g * gw:(g + 1) * gw]
        yb = yb * z_gate[:, g * gw:(g + 1) * gw]
        yb = _rms(yb, sg_ref[:, g * gw:(g + 1) * gw])
        lo = RET_HEADS * RET_DV + g * gw
        y_ref[:, lo:lo + gw] = yb.astype(y_ref.dtype)


def _even_core(p, px, dt, cos, sin, qn, kn, gn, dtb, alog, dsk, sg, e_mat, tril):
    b, t, _ = p.shape
    c = CHUNK
    full = lambda a: pl.BlockSpec(a.shape, lambda bi, ti: (0,) * a.ndim)
    smalls = (qn, kn, gn, dtb, alog, dsk, sg, e_mat, tril)
    return pl.pallas_call(
        _even_core_kernel,
        grid=(b, t // c),
        in_specs=[
            pl.BlockSpec((None, c, EV_XBC), lambda bi, ti: (bi, ti, 0)),
            pl.BlockSpec((None, c, SSD_CONV_DIM), lambda bi, ti: (bi, ti, 0)),
            pl.BlockSpec((None, c, EVEN_IN_PAD - EV_DT), lambda bi, ti: (bi, ti, 0)),
            pl.BlockSpec((c, RET_DK), lambda bi, ti: (ti, 0)),
            pl.BlockSpec((c, RET_DK), lambda bi, ti: (ti, 0)),
        ] + [full(a) for a in smalls],
        out_specs=pl.BlockSpec((None, c, EVEN_OUT), lambda bi, ti: (bi, ti, 0)),
        out_shape=jax.ShapeDtypeStruct((b, t, EVEN_OUT), BF16),
        scratch_shapes=[
            pltpu.VMEM((RET_HEADS, RET_DK, RET_DV), F32),
            pltpu.VMEM((SSD_GROUPS, SSD_STATE, SSD_D_INNER // SSD_GROUPS), F32),
            pltpu.VMEM((RET_HEADS, c, c), F32),
        ],
        compiler_params=_params(("parallel", "arbitrary")),
        name="even_core",
    )(p, px, dt, cos, sin, *smalls)


def _odd_core_kernel(pq_ref, pc_ref, wg_ref, ba_ref, bx_ref, lam_ref, qn_ref, kn_ref, mavg_ref,
                     yc_ref, q_ref, k_ref, v_ref, a_scr, b_scr, hcarry):
    t = pl.program_id(1)
    r = pc_ref.shape[0]

    @pl.when(t == 0)
    def _init():
        hcarry[...] = jnp.zeros_like(hcarry)

    xc = pc_ref[...].astype(F32)

    rs, is_ = [], []
    for blk in range(LRU_BLOCKS):
        g = _dot(xc[:, blk * LRU_BLOCK:(blk + 1) * LRU_BLOCK].astype(BF16), wg_ref[blk])
        rs.append(g[:, :LRU_BLOCK])
        is_.append(g[:, LRU_BLOCK:])
    rg = _sigmoid(jnp.concatenate(rs, axis=1) + ba_ref[...])
    ig = _sigmoid(jnp.concatenate(is_, axis=1) + bx_ref[...])
    log_a = -LRU_C * rg * _softplus(-lam_ref[...])
    a_scr[...] = jnp.exp(log_a)
    b_scr[...] = jnp.sqrt(1.0 - jnp.exp(2.0 * log_a)) * (ig * xc)

    rowi = lax.broadcasted_iota(jnp.int32, (SUBLANES, LRU_WIDTH), 0)

    def body(gi, carry):
        off = pl.multiple_of(gi * SUBLANES, SUBLANES)
        a = a_scr[pl.ds(off, SUBLANES), :]
        bv = b_scr[pl.ds(off, SUBLANES), :]
        for s in (1, 2, 4):
            a_sh = jnp.where(rowi >= s, pltpu.roll(a, s, 0), 1.0)
            b_sh = jnp.where(rowi >= s, pltpu.roll(bv, s, 0), 0.0)
            bv = a * b_sh + bv
            a = a * a_sh
        h = a * carry + bv
        b_scr[pl.ds(off, SUBLANES), :] = h
        return jnp.broadcast_to(h[SUBLANES - 1:SUBLANES, :], (SUBLANES, LRU_WIDTH))

    hcarry[...] = lax.fori_loop(0, r // SUBLANES, body, hcarry[...])
    yc_ref[...] = (b_scr[...] * pq_ref[:, 3 * SB_WIDTH:].astype(F32)).astype(yc_ref.dtype)

    def headnorm(x, g):
        sq = x * x
        hi = sq.astype(BF16)
        lo = (sq - hi.astype(F32)).astype(BF16)
        ms = _dot(hi, mavg_ref[...]) + _dot(lo, mavg_ref[...])
        return x * lax.rsqrt(ms + EPS) * g

    q = headnorm(pq_ref[:, 0:SB_WIDTH].astype(F32), qn_ref[...]) * (SB_HEAD_DIM ** -0.5 * LOG2E)
    q_ref[...] = q.astype(q_ref.dtype)
    kn = headnorm(pq_ref[:, SB_WIDTH:2 * SB_WIDTH].astype(F32), kn_ref[...])
    vt = pq_ref[:, 2 * SB_WIDTH:3 * SB_WIDTH].astype(F32).T
    pw = 2 * SB_HEAD_DIM
    lane = lax.broadcasted_iota(jnp.int32, (r, pw), 1)
    sub = lax.broadcasted_iota(jnp.int32, (pw, r), 0)
    for pr in range(SB_HEADS // 2):
        kp = kn[:, pr * pw:(pr + 1) * pw]
        k_ref[:, 2 * pr * pw:(2 * pr + 1) * pw] = jnp.where(lane < SB_HEAD_DIM, kp, 0.0).astype(k_ref.dtype)
        k_ref[:, (2 * pr + 1) * pw:(2 * pr + 2) * pw] = jnp.where(lane < SB_HEAD_DIM, 0.0, kp).astype(k_ref.dtype)
        vp = vt[pr * pw:(pr + 1) * pw, :]
        v_ref[2 * pr * pw:(2 * pr + 1) * pw, :] = jnp.where(sub < SB_HEAD_DIM, vp, 0.0).astype(v_ref.dtype)
        v_ref[(2 * pr + 1) * pw:(2 * pr + 2) * pw, :] = jnp.where(sub < SB_HEAD_DIM, 0.0, vp).astype(v_ref.dtype)


def _odd_core(pq, pc, wg, ba, bx, lam, qn, kn, mavg):
    b, t, _ = pq.shape
    r = min(256, t)
    full = lambda a: pl.BlockSpec(a.shape, lambda bi, ti: (0,) * a.ndim)
    smalls = (wg, ba, bx, lam, qn, kn, mavg)
    blk = lambda w: pl.BlockSpec((None, r, w), lambda bi, ti: (bi, ti, 0))
    return pl.pallas_call(
        _odd_core_kernel,
        grid=(b, t // r),
        in_specs=[blk(3 * SB_WIDTH + LRU_WIDTH), blk(LRU_WIDTH)] + [full(a) for a in smalls],
        out_specs=[blk(LRU_WIDTH), blk(SB_WIDTH), blk(2 * SB_WIDTH),
                   pl.BlockSpec((None, 2 * SB_WIDTH, r), lambda bi, ti: (bi, 0, ti))],
        out_shape=[jax.ShapeDtypeStruct((b, t, LRU_WIDTH), BF16), jax.ShapeDtypeStruct((b, t, SB_WIDTH), BF16),
                   jax.ShapeDtypeStruct((b, t, 2 * SB_WIDTH), BF16), jax.ShapeDtypeStruct((b, 2 * SB_WIDTH, t), BF16)],
        scratch_shapes=[
            pltpu.VMEM((r, LRU_WIDTH), F32),
            pltpu.VMEM((r, LRU_WIDTH), F32),
            pltpu.VMEM((SUBLANES, LRU_WIDTH), F32),
        ],
        compiler_params=_params(("parallel", "arbitrary")),
        name="odd_core",
    )(pq, pc, *smalls)


SB_TQ = 512
SB_TK = 128
SB_UNROLL = 2
SB_STAGES = 5


def _sb_attn_kernel(q_ref, k_ref, vt_ref, ut_ref, o_ref, z_buf, lb_buf, l1b_buf, r0_buf, r0_prev, c_buf, sfx_buf, w_buf, acc):
    qi = pl.program_id(2)
    tq = q_ref.shape[0]
    tk = SB_TK
    pw = 2 * SB_HEAD_DIM
    ndiag = tq // tk
    jmax = qi * ndiag + (ndiag - 1)
    kloc = lax.broadcasted_iota(jnp.int32, (tk, tq), 0)
    qpos = qi * tq + lax.broadcasted_iota(jnp.int32, (tk, tq), 1)

    def key_start(t):
        return pl.multiple_of((jmax - t) * tk, tk)

    def p0(t):
        kx = k_ref[pl.ds(key_start(t), tk), :]
        km = jnp.concatenate([kx[:, :pw], kx[:, pw:]], axis=0)
        z_buf[...] = _dot_nt(km, q_ref[...])

    def p1(t, slot, masked):
        if masked:
            valid = (key_start(t) + kloc) < qpos
        for h in range(2):
            z = z_buf[h * tk:(h + 1) * tk, :]
            neg_abs = lax.bitcast_convert_type(lax.bitcast_convert_type(z, jnp.int32) | SIGN_BIT, F32)
            sp = jnp.log(1.0 + jnp.exp2(neg_abs)) * LOG2E
            lb = jnp.minimum(z, 0.0) - sp
            l1 = lb - z
            if masked:
                l1 = jnp.where(valid, l1, 0.0)
            lb_buf[slot, h * tk:(h + 1) * tk, :] = lb
            l1b_buf[:, h * tq:(h + 1) * tq] = l1.astype(BF16)
            r0_buf[:, h * tq:(h + 1) * tq] = l1[0:1, :]

    def p2():
        sfx_buf[...] = _dot(ut_ref[...], l1b_buf[...])
        r0_prev[...] = r0_buf[...]

    def p3(t, slot, masked):
        if masked:
            valid = (key_start(t) + kloc) < qpos
        carry = c_buf[...]
        for h in range(2):
            sfx = sfx_buf[:, h * tq:(h + 1) * tq] + carry[:, h * tq:(h + 1) * tq]
            w = jnp.exp2(lb_buf[slot, h * tk:(h + 1) * tk, :] + sfx)
            if masked:
                w = jnp.where(valid, w, 0.0)
            w_buf[h * tk:(h + 1) * tk, :] = w.astype(BF16)
        c_buf[...] = carry + sfx_buf[0:1, :] + r0_prev[...]

    def p4(t):
        vx = vt_ref[:, pl.ds(key_start(t), tk)]
        vbd = jnp.concatenate([vx[:pw], vx[pw:]], axis=1)
        acc[...] += _dot(vbd, w_buf[...])

    def emit(i, active, parity, masked):
        if 4 in active:
            p4(i - 4)
        if 3 in active:
            p3(i - 3, (parity - 3) % 2, masked(i - 3))
        if 2 in active:
            p2()
        if 1 in active:
            p1(i - 1, (parity - 1) % 2, masked(i - 1))
        if 0 in active:
            p0(i)

    def static_iterations(n_tiles, iters):
        for i in iters:
            active = {k for k in range(SB_STAGES) if 0 <= i - k < n_tiles}
            emit(i, active, i % 2, lambda t: t < ndiag)

    acc[...] = jnp.zeros_like(acc)
    c_buf[...] = jnp.zeros_like(c_buf)

    @pl.when(qi == 0)
    def _diagonal_only():
        static_iterations(ndiag, range(ndiag + SB_STAGES - 1))

    @pl.when(qi > 0)
    def _full():
        n = (qi + 1) * ndiag
        head = 2 * ndiag
        static_iterations(head, range(head))
        never = lambda t: False

        def body(jj, _):
            for u in range(SB_UNROLL):
                emit(head + SB_UNROLL * jj + u, set(range(SB_STAGES)), u % 2, never)
            return 0

        lax.fori_loop(0, (n - head) // SB_UNROLL, body, 0)
        for e in range(SB_STAGES - 1):
            emit(n + e, {k for k in range(SB_STAGES) if k > e}, e % 2, never)

    o_ref[...] = acc[...].T.astype(o_ref.dtype)


def _sb_attention(q, kx, vtx, ut):
    b, t, _ = q.shape
    tq = min(SB_TQ, t)
    tk = SB_TK
    pairs = SB_HEADS // 2
    pw = 2 * SB_HEAD_DIM
    return pl.pallas_call(
        _sb_attn_kernel,
        grid=(b, pairs, t // tq),
        in_specs=[
            pl.BlockSpec((None, tq, pw), lambda bi, pi, qi: (bi, qi, pi)),
            pl.BlockSpec((None, t, 2 * pw), lambda bi, pi, qi: (bi, 0, pi)),
            pl.BlockSpec((None, 2 * pw, t), lambda bi, pi, qi: (bi, pi, 0)),
            pl.BlockSpec(ut.shape, lambda bi, pi, qi: (0, 0)),
        ],
        out_specs=pl.BlockSpec((None, tq, pw), lambda bi, pi, qi: (bi, qi, pi)),
        out_shape=jax.ShapeDtypeStruct((b, t, SB_WIDTH), BF16),
        scratch_shapes=[
            pltpu.VMEM((2 * tk, tq), F32),
            pltpu.VMEM((2, 2 * tk, tq), F32),
            pltpu.VMEM((tk, 2 * tq), BF16),
            pltpu.VMEM((1, 2 * tq), F32),
            pltpu.VMEM((1, 2 * tq), F32),
            pltpu.VMEM((1, 2 * tq), F32),
            pltpu.VMEM((tk, 2 * tq), F32),
            pltpu.VMEM((2 * tk, tq), BF16),
            pltpu.VMEM((pw, tq), F32),
        ],
        compiler_params=_params(("parallel", "parallel", "arbitrary")),
        name="sb_attention",
    )(q, kx, vtx, ut)


def _const_tables():
    i = np.arange(128)
    tril = (i[None, :] <= i[:, None]).astype(np.float32)
    e_mat = np.zeros((128, SSD_D_INNER), np.float32)
    for h in range(SSD_HEADS):
        e_mat[h, h * SSD_HEADDIM:(h + 1) * SSD_HEADDIM] = 1.0
    uo = (i[None, :] > i[:, None]).astype(np.float32)
    hd = np.arange(SB_WIDTH) // SB_HEAD_DIM
    mavg = (hd[:, None] == hd[None, :]).astype(np.float32) / SB_HEAD_DIM
    return (jnp.asarray(tril, BF16), jnp.asarray(e_mat, BF16), jnp.asarray(uo, BF16), jnp.asarray(mavg, BF16))


def _row(v):
    return v.reshape(1, -1).astype(F32)


def _pad_lanes(v, width=128):
    return jnp.pad(v.reshape(1, -1).astype(F32), ((0, 0), (0, width - v.shape[-1])))


def kernel(x, norm_mix, norm_mlp, mlp_w1, mlp_w2, ev_w_in, ev_w_out, ret_qn, ret_kn, ret_gn, ssd_conv_w, ssd_conv_b, ssd_dt_bias, ssd_a_log, ssd_d, ssd_norm, od_w_in, od_w_out, lru_conv_w, lru_conv_b, lru_wa, lru_ba, lru_wx, lru_bx, lru_lam, sb_qn, sb_kn):
    b, t, d = x.shape
    n = b * t
    depth = norm_mix.shape[0]
    tril, e_mat, uo, mavg = _const_tables()
    cos, sin = _rope_tables(t)
    xf = x.reshape(n, d)
    for l in range(depth):
        g_mix = _row(norm_mix[l])
        if l % 2 == 0:
            e = l // 2
            w_in = jnp.pad(ev_w_in[e], ((0, 0), (0, EVEN_IN_PAD - EVEN_IN))).astype(BF16)
            p = _norm_proj_act(xf, g_mix, w_in[:, :EV_XBC], 1024, "silu", EV_G // 1024)
            px, dt = _norm_proj_conv(xf, g_mix, w_in[:, EV_XBC:], ssd_conv_w[e].astype(F32), _row(ssd_conv_b[e]), True, t)
            y = _even_core(
                p.reshape(b, t, EV_XBC), px.reshape(b, t, SSD_CONV_DIM), dt.reshape(b, t, EVEN_IN_PAD - EV_DT), cos, sin,
                _row(ret_qn[e]), _row(ret_kn[e]), _row(ret_gn[e]), _pad_lanes(ssd_dt_bias[e]), _pad_lanes(ssd_a_log[e]),
                _row(jnp.repeat(ssd_d[e], SSD_HEADDIM)), _row(ssd_norm[e]), e_mat, tril)
            xf = _outproj(xf, [y.reshape(n, EVEN_OUT)], [ev_w_out[e].astype(BF16)])
        else:
            o = l // 2
            w_in = od_w_in[o].astype(BF16)
            w_qg = jnp.concatenate([w_in[:, OD_Q:], w_in[:, :OD_XC]], axis=1)
            pq = _norm_proj_act(xf, g_mix, w_qg, SB_WIDTH, "gelu", 3)
            (pc,) = _norm_proj_conv(xf, g_mix, w_in[:, OD_XC:OD_Q], lru_conv_w[o].astype(F32), _row(lru_conv_b[o]), False, t)
            wg = jnp.concatenate([lru_wa[o], lru_wx[o]], axis=-1).astype(BF16)
            yc, q, k, v = _odd_core(
                pq.reshape(b, t, 3 * SB_WIDTH + LRU_WIDTH), pc.reshape(b, t, LRU_WIDTH),
                wg, _row(lru_ba[o]), _row(lru_bx[o]),
                _row(lru_lam[o]), _row(jnp.tile(sb_qn[o], SB_HEADS)), _row(jnp.tile(sb_kn[o], SB_HEADS)), mavg)
            yd = _sb_attention(q, k, v, uo)
            w_out = od_w_out[o].astype(BF16)
            xf = _outproj(xf, [yc.reshape(n, LRU_WIDTH), yd.reshape(n, SB_WIDTH)], [w_out[:LRU_WIDTH], w_out[LRU_WIDTH:]])
        xf = _mlp(xf, _row(norm_mlp[l]), mlp_w1[l].astype(BF16), mlp_w2[l].astype(BF16))
    return xf.reshape(b, t, d)
```

```python
import functools
import math

import numpy as np
import jax
import jax.numpy as jnp
from jax import lax
from jax.experimental import pallas as pl
from jax.experimental.pallas import tpu as pltpu

F32 = jnp.float32
BF16 = jnp.bfloat16

D_MODEL = 1024
EPS = 1e-6

RET_HEADS = 4
RET_DK = 128
RET_DV = 256
ROPE_BASE = 10000.0
RET_LOG_GAMMA = tuple(float(np.log1p(-(2.0 ** (-5.0 - h)))) for h in range(RET_HEADS))

SSD_D_INNER = 1024
SSD_HEADDIM = 64
SSD_HEADS = 16
SSD_GROUPS = 2
SSD_STATE = 128
SSD_CONV = 4
SSD_CONV_DIM = SSD_D_INNER + 2 * SSD_GROUPS * SSD_STATE
CHUNK = 128

LRU_WIDTH = 1024
LRU_BLOCKS = 8
LRU_BLOCK = 128
LRU_C = 8.0
LRU_CONV = 4

SB_HEADS = 8
SB_HEAD_DIM = 64
SB_WIDTH = SB_HEADS * SB_HEAD_DIM

D_FF = 4 * D_MODEL

EVEN_IN = 5648
EVEN_IN_PAD = 5760
EVEN_OUT = 2048
ODD_IN = 3584
ODD_OUT = 1536

EV_Q, EV_K, EV_V, EV_G, EV_Z, EV_XBC, EV_DT = 0, 512, 1024, 2048, 3072, 4096, 5632
OD_GATE, OD_XC, OD_Q, OD_K, OD_V = 0, 1024, 2048, 2560, 3072

LOG2E = 1.4426950408889634
SIGN_BIT = np.int32(-2 ** 31)
SUBLANES = 8
CONV_PAD = 8
VMEM_LIMIT = 56 * 1024 * 1024


def _params(sem):
    return pltpu.CompilerParams(dimension_semantics=sem, vmem_limit_bytes=VMEM_LIMIT)


def _dot(a, b):
    return jnp.dot(a, b, preferred_element_type=F32)


def _dot_nt(a, b):
    return lax.dot_general(a, b, (((1,), (1,)), ((), ())), preferred_element_type=F32)


def _dot_tn(a, b):
    return lax.dot_general(a, b, (((0,), (0,)), ((), ())), preferred_element_type=F32)


def _split3(x):
    hi = x.astype(BF16)
    r = x - hi.astype(F32)
    mid = r.astype(BF16)
    lo = (r - mid.astype(F32)).astype(BF16)
    return hi, mid, lo


def _dot_exact_lhs(x, m):
    hi, mid, lo = _split3(x)
    return _dot(hi, m) + _dot(mid, m) + _dot(lo, m)


def _dot_exact_rhs(m, x):
    hi, mid, lo = _split3(x)
    return _dot(m, hi) + _dot(m, mid) + _dot(m, lo)


def _sigmoid(x):
    return 1.0 / (1.0 + jnp.exp(-x))


def _silu(x):
    return x * _sigmoid(x)


def _softplus(x):
    return jnp.maximum(x, 0.0) + jnp.log(1.0 + jnp.exp(-jnp.abs(x)))


def _gelu_tanh(x):
    c = math.sqrt(2.0 / math.pi)
    return x * (0.5 * (1.0 + jnp.tanh(c * (x + 0.044715 * (x * x * x)))))


def _rms(x, g):
    return x * lax.rsqrt(jnp.mean(x * x, axis=-1, keepdims=True) + EPS) * g


def _causal_conv(acc, cw_ref, cb_ref, xpad, tail, reset):
    rows = acc.shape[0]
    taps = cw_ref.shape[0]
    xpad[0:CONV_PAD, :] = jnp.where(reset, 0.0, tail[...])
    xpad[CONV_PAD:CONV_PAD + rows, :] = acc
    y = cb_ref[...]
    for k in range(taps):
        off = CONV_PAD - (taps - 1) + k
        y = y + cw_ref[k:k + 1, :] * xpad[off:off + rows, :]
    tail[...] = xpad[rows:rows + CONV_PAD, :]
    return y


_ACTS = {"none": lambda v: v, "silu": _silu, "gelu": _gelu_tanh}
PROJ_TM = 512


def _norm_proj_kernel(plan, n_out, tiles_per_seq, x_ref, g_ref, w_ref, cw_ref, cb_ref, *rest):
    outs, (xpad, tail) = rest[:n_out], rest[n_out:]
    h = _rms(x_ref[...], g_ref[...]).astype(BF16)
    for lo, hi, op, out, dst in plan:
        acc = _dot(h, w_ref[:, lo:hi])
        if op.startswith("conv"):
            acc = _causal_conv(acc, cw_ref, cb_ref, xpad, tail, pl.program_id(0) % tiles_per_seq == 0)
            op = "silu" if op == "conv_silu" else "none"
        outs[out][:, dst:dst + hi - lo] = _ACTS[op](acc).astype(outs[out].dtype)


def _norm_proj(x, g, w, cw, cb, plan, out_widths, out_dtypes, seq_len):
    n, d = x.shape
    tm = min(PROJ_TM, n)
    row = lambda width: pl.BlockSpec((tm, width), lambda i: (i, 0))
    full = lambda a: pl.BlockSpec(a.shape, lambda i: (0, 0))
    resident = pl.BlockSpec(w.shape, lambda i: (0, 0), pipeline_mode=pl.Buffered(1))
    cwidth = cw.shape[1]
    return pl.pallas_call(
        functools.partial(_norm_proj_kernel, plan, len(out_widths), seq_len // tm),
        grid=(n // tm,),
        in_specs=[row(d), full(g), resident, full(cw), full(cb)],
        out_specs=[row(wd) for wd in out_widths],
        out_shape=[jax.ShapeDtypeStruct((n, wd), dt) for wd, dt in zip(out_widths, out_dtypes)],
        scratch_shapes=[pltpu.VMEM((tm + CONV_PAD, cwidth), F32), pltpu.VMEM((CONV_PAD, cwidth), F32)],
        compiler_params=_params(("arbitrary",)),
        name="norm_proj",
    )(x, g, w, cw, cb)


def _mlp_kernel(x_ref, g_ref, w1_ref, w2_ref, o_ref, h_scr, acc_scr):
    f = pl.program_id(1)

    @pl.when(f == 0)
    def _():
        h_scr[...] = _rms(x_ref[...], g_ref[...]).astype(BF16)
        acc_scr[...] = jnp.zeros_like(acc_scr)

    a = _dot(h_scr[...], w1_ref[...])
    a = jnp.square(jnp.maximum(a, 0.0)).astype(BF16)
    acc_scr[...] += _dot(a, w2_ref[...])

    @pl.when(f == pl.num_programs(1) - 1)
    def _():
        o_ref[...] = x_ref[...] + acc_scr[...]


def _mlp(x, g, w1, w2):
    n, d = x.shape
    dff = w1.shape[1]
    tm = min(1024, n)
    tf = 512
    return pl.pallas_call(
        _mlp_kernel,
        grid=(n // tm, dff // tf),
        in_specs=[
            pl.BlockSpec((tm, d), lambda i, f: (i, 0)),
            pl.BlockSpec((1, d), lambda i, f: (0, 0)),
            pl.BlockSpec((d, tf), lambda i, f: (0, f)),
            pl.BlockSpec((tf, d), lambda i, f: (f, 0)),
        ],
        out_specs=pl.BlockSpec((tm, d), lambda i, f: (i, 0)),
        out_shape=jax.ShapeDtypeStruct((n, d), F32),
        scratch_shapes=[pltpu.VMEM((tm, d), BF16), pltpu.VMEM((tm, d), F32)],
        compiler_params=_params(("parallel", "arbitrary")),
        name="mlp",
    )(x, g, w1, w2)


def _outproj1_kernel(x_ref, y_ref, w_ref, o_ref):
    o_ref[...] = x_ref[...] + _dot(y_ref[...], w_ref[...])


def _outproj2_kernel(x_ref, ya_ref, yb_ref, wa_ref, wb_ref, o_ref):
    o_ref[...] = x_ref[...] + _dot(ya_ref[...], wa_ref[...]) + _dot(yb_ref[...], wb_ref[...])


def _outproj(x, ys, ws):
    n, d = x.shape
    tm = min(512, n)
    kern = _outproj1_kernel if len(ys) == 1 else _outproj2_kernel
    in_specs = [pl.BlockSpec((tm, d), lambda i: (i, 0))]
    in_specs += [pl.BlockSpec((tm, y.shape[1]), lambda i: (i, 0)) for y in ys]
    in_specs += [pl.BlockSpec(w.shape, lambda i: (0, 0)) for w in ws]
    return pl.pallas_call(
        kern,
        grid=(n // tm,),
        in_specs=in_specs,
        out_specs=pl.BlockSpec((tm, d), lambda i: (i, 0)),
        out_shape=jax.ShapeDtypeStruct((n, d), F32),
        compiler_params=_params(("parallel",)),
        name="outproj",
    )(x, *ys, *ws)


def _rope_kernel(inv_ref, cos_ref, sin_ref):
    rows = cos_ref.shape[0]
    pos = (pl.program_id(0) * rows + lax.broadcasted_iota(jnp.int32, (rows, RET_DK), 0)).astype(F32)
    lane = lax.broadcasted_iota(jnp.int32, (rows, RET_DK), 1)
    ang = pos * inv_ref[...]
    cos_ref[...] = jnp.cos(ang)
    sin_ref[...] = jnp.where(lane < RET_DK // 2, -jnp.sin(ang), jnp.sin(ang))


def _rope_tables(t):
    half = RET_DK // 2
    inv = ROPE_BASE ** (-jnp.arange(half, dtype=F32) / half)
    inv = jnp.concatenate([inv, inv])[None, :]
    rows = min(512, t)
    return pl.pallas_call(
        _rope_kernel,
        grid=(t // rows,),
        in_specs=[pl.BlockSpec((1, RET_DK), lambda i: (0, 0))],
        out_specs=[pl.BlockSpec((rows, RET_DK), lambda i: (i, 0))] * 2,
        out_shape=[jax.ShapeDtypeStruct((t, RET_DK), F32)] * 2,
        compiler_params=_params(("parallel",)),
        name="rope_tables",
    )(inv)


def _even_core_kernel(p_ref, px_ref, dt_ref, cos_ref, sin_ref, qn_ref, kn_ref, gn_ref, dtb_ref, alog_ref,
                      dsk_ref, sg_ref, e_ref, tril_ref, y_ref, rstate, sstate, decay_scr):
    t = pl.program_id(1)
    c = CHUNK
    row = lax.broadcasted_iota(jnp.int32, (c, c), 0)
    col = lax.broadcasted_iota(jnp.int32, (c, c), 1)
    causal = row >= col

    @pl.when(t == 0)
    def _init():
        rstate[...] = jnp.zeros_like(rstate)
        sstate[...] = jnp.zeros_like(sstate)
        rel = (row - col).astype(F32)
        for h in range(RET_HEADS):
            decay_scr[h] = jnp.where(causal, jnp.exp(RET_LOG_GAMMA[h] * jnp.maximum(rel, 0.0)), 0.0)

    cos = cos_ref[...]
    sin = sin_ref[...]
    idx = lax.broadcasted_iota(jnp.int32, (c, 1), 0).astype(F32)
    for h in range(RET_HEADS):
        lg = RET_LOG_GAMMA[h]
        qh = _rms(p_ref[:, EV_Q + h * RET_DK:EV_Q + (h + 1) * RET_DK].astype(F32), qn_ref[...])
        kh = _rms(p_ref[:, EV_K + h * RET_DK:EV_K + (h + 1) * RET_DK].astype(F32), kn_ref[...])
        qh = qh * cos + pltpu.roll(qh, RET_DK // 2, 1) * sin
        kh = (kh * cos + pltpu.roll(kh, RET_DK // 2, 1) * sin) * (RET_DK ** -0.5)
        vb = p_ref[:, EV_V + h * RET_DV:EV_V + (h + 1) * RET_DV]
        scores = _dot_nt(qh.astype(BF16), kh.astype(BF16)) * decay_scr[h]
        inner = _dot(scores.astype(BF16), vb)
        s_prev = rstate[h]
        q_dec = jnp.exp(lg * (idx + 1.0))
        cross = _dot((qh * q_dec).astype(BF16), s_prev.astype(BF16))
        k_dec = jnp.exp(lg * (float(c - 1) - idx))
        rstate[h] = s_prev * math.exp(lg * c) + _dot_tn((kh * k_dec).astype(BF16), vb)
        ya = inner + cross
        yc = ya - jnp.mean(ya, axis=-1, keepdims=True)
        yn = yc * lax.rsqrt(jnp.mean(yc * yc, axis=-1, keepdims=True) + EPS)
        yn = yn * gn_ref[:, h * RET_DV:(h + 1) * RET_DV]
        gate = p_ref[:, EV_G + h * RET_DV:EV_G + (h + 1) * RET_DV].astype(F32)
        y_ref[:, h * RET_DV:(h + 1) * RET_DV] = (gate * yn).astype(y_ref.dtype)

    xs = px_ref[:, :SSD_D_INNER].astype(F32)
    gs = SSD_GROUPS * SSD_STATE
    bm = px_ref[:, SSD_D_INNER:SSD_D_INNER + gs]
    cm = px_ref[:, SSD_D_INNER + gs:SSD_D_INNER + 2 * gs]

    dt = _softplus(dt_ref[...] + dtb_ref[...])
    a_neg = -jnp.exp(alog_ref[...])
    d_a = dt * a_neg
    acum = _dot_exact_rhs(tril_ref[...], d_a)
    acum_t = acum.T
    exp_a = jnp.exp(acum)
    dec = jnp.exp(acum[c - 1:c, :] - acum)
    e_mat = e_ref[...]
    dt_e = _dot_exact_lhs(dt, e_mat)
    exp_a_e = _dot_exact_lhs(exp_a, e_mat)
    dec_e = _dot_exact_lhs(dec, e_mat)
    xr = xs * dt_e
    xrd_b = (xr * dec_e).astype(BF16)
    cdec_e = exp_a_e[c - 1:c, :]
    lane = lax.broadcasted_iota(jnp.int32, (c, 128), 1)
    left = lane < SSD_HEADDIM
    hpg = SSD_HEADS // SSD_GROUPS
    gw = hpg * SSD_HEADDIM
    z_gate = p_ref[:, EV_Z:EV_Z + SSD_D_INNER].astype(F32)
    for g in range(SSD_GROUPS):
        bg = bm[:, g * SSD_STATE:(g + 1) * SSD_STATE]
        cg = cm[:, g * SSD_STATE:(g + 1) * SSD_STATE]
        cb_mat = _dot_nt(cg, bg)
        s_prev = sstate[g]
        y_off = _dot(cg, s_prev.astype(BF16)) * exp_a_e[:, g * gw:(g + 1) * gw]
        pairs = []
        for pr in range(hpg // 2):
            h0 = g * hpg + 2 * pr
            ms = []
            for h in (h0, h0 + 1):
                seg = acum[:, h:h + 1] - acum_t[h:h + 1, :]
                l_mat = jnp.where(causal, jnp.exp(jnp.minimum(seg, 0.0)), 0.0)
                ms.append((cb_mat * l_mat).astype(BF16))
            xp = xr[:, h0 * SSD_HEADDIM:(h0 + 2) * SSD_HEADDIM]
            xbd = jnp.concatenate([jnp.where(left, xp, 0.0), jnp.where(left, 0.0, xp)], axis=0).astype(BF16)
            pairs.append(_dot(jnp.concatenate(ms, axis=1), xbd))
        y_diag = jnp.concatenate(pairs, axis=1)
        sstate[g] = s_prev * cdec_e[:, g * gw:(g + 1) * gw] + _dot_tn(bg, xrd_b[:, g * gw:(g + 1) * gw])
        yb = y_diag + y_off + xs[:, g * gw:(g + 1) * gw] * dsk_ref[:, g * gw:(g + 1) * gw]
        yb = yb * z_gate[:, g * gw:(g + 1) * gw]
        yb = _rms(yb, sg_ref[:, g * gw:(g + 1) * gw])
        lo = RET_HEADS * RET_DV + g * gw
        y_ref[:, lo:lo + gw] = yb.astype(y_ref.dtype)


def _even_core(p, px, dt, cos, sin, qn, kn, gn, dtb, alog, dsk, sg, e_mat, tril):
    b, t, _ = p.shape
    c = CHUNK
    full = lambda a: pl.BlockSpec(a.shape, lambda bi, ti: (0,) * a.ndim)
    smalls = (qn, kn, gn, dtb, alog, dsk, sg, e_mat, tril)
    return pl.pallas_call(
        _even_core_kernel,
        grid=(b, t // c),
        in_specs=[
            pl.BlockSpec((None, c, EV_XBC), lambda bi, ti: (bi, ti, 0)),
            pl.BlockSpec((None, c, SSD_CONV_DIM), lambda bi, ti: (bi, ti, 0)),
            pl.BlockSpec((None, c, EVEN_IN_PAD - EV_DT), lambda bi, ti: (bi, ti, 0)),
            pl.BlockSpec((c, RET_DK), lambda bi, ti: (ti, 0)),
            pl.BlockSpec((c, RET_DK), lambda bi, ti: (ti, 0)),
        ] + [full(a) for a in smalls],
        out_specs=pl.BlockSpec((None, c, EVEN_OUT), lambda bi, ti: (bi, ti, 0)),
        out_shape=jax.ShapeDtypeStruct((b, t, EVEN_OUT), BF16),
        scratch_shapes=[
            pltpu.VMEM((RET_HEADS, RET_DK, RET_DV), F32),
            pltpu.VMEM((SSD_GROUPS, SSD_STATE, SSD_D_INNER // SSD_GROUPS), F32),
            pltpu.VMEM((RET_HEADS, c, c), F32),
        ],
        compiler_params=_params(("parallel", "arbitrary")),
        name="even_core",
    )(p, px, dt, cos, sin, *smalls)


def _odd_core_kernel(pq_ref, pc_ref, wg_ref, ba_ref, bx_ref, lam_ref, qn_ref, kn_ref, mavg_ref,
                     yc_ref, q_ref, k_ref, v_ref, a_scr, b_scr, hcarry):
    t = pl.program_id(1)
    r = pc_ref.shape[0]

    @pl.when(t == 0)
    def _init():
        hcarry[...] = jnp.zeros_like(hcarry)

    xc = pc_ref[...].astype(F32)

    rs, is_ = [], []
    for blk in range(LRU_BLOCKS):
        g = _dot(xc[:, blk * LRU_BLOCK:(blk + 1) * LRU_BLOCK].astype(BF16), wg_ref[blk])
        rs.append(g[:, :LRU_BLOCK])
        is_.append(g[:, LRU_BLOCK:])
    rg = _sigmoid(jnp.concatenate(rs, axis=1) + ba_ref[...])
    ig = _sigmoid(jnp.concatenate(is_, axis=1) + bx_ref[...])
    log_a = -LRU_C * rg * _softplus(-lam_ref[...])
    a_scr[...] = jnp.exp(log_a)
    b_scr[...] = jnp.sqrt(1.0 - jnp.exp(2.0 * log_a)) * (ig * xc)

    rowi = lax.broadcasted_iota(jnp.int32, (SUBLANES, LRU_WIDTH), 0)

    def body(gi, carry):
        off = pl.multiple_of(gi * SUBLANES, SUBLANES)
        a = a_scr[pl.ds(off, SUBLANES), :]
        bv = b_scr[pl.ds(off, SUBLANES), :]
        for s in (1, 2, 4):
            a_sh = jnp.where(rowi >= s, pltpu.roll(a, s, 0), 1.0)
            b_sh = jnp.where(rowi >= s, pltpu.roll(bv, s, 0), 0.0)
            bv = a * b_sh + bv
            a = a * a_sh
        h = a * carry + bv
        b_scr[pl.ds(off, SUBLANES), :] = h
        return jnp.broadcast_to(h[SUBLANES - 1:SUBLANES, :], (SUBLANES, LRU_WIDTH))

    hcarry[...] = lax.fori_loop(0, r // SUBLANES, body, hcarry[...])
    yc_ref[...] = (b_scr[...] * pq_ref[:, 3 * SB_WIDTH:].astype(F32)).astype(yc_ref.dtype)

    def headnorm(x, g):
        sq = x * x
        hi = sq.astype(BF16)
        lo = (sq - hi.astype(F32)).astype(BF16)
        ms = _dot(hi, mavg_ref[...]) + _dot(lo, mavg_ref[...])
        return x * lax.rsqrt(ms + EPS) * g

    q = headnorm(pq_ref[:, 0:SB_WIDTH].astype(F32), qn_ref[...]) * (SB_HEAD_DIM ** -0.5 * LOG2E)
    q_ref[...] = q.astype(q_ref.dtype)
    kn = headnorm(pq_ref[:, SB_WIDTH:2 * SB_WIDTH].astype(F32), kn_ref[...])
    vt = pq_ref[:, 2 * SB_WIDTH:3 * SB_WIDTH].astype(F32).T
    pw = 2 * SB_HEAD_DIM
    lane = lax.broadcasted_iota(jnp.int32, (r, pw), 1)
    sub = lax.broadcasted_iota(jnp.int32, (pw, r), 0)
    for pr in range(SB_HEADS // 2):
        kp = kn[:, pr * pw:(pr + 1) * pw]
        k_ref[:, 2 * pr * pw:(2 * pr + 1) * pw] = jnp.where(lane < SB_HEAD_DIM, kp, 0.0).astype(k_ref.dtype)
        k_ref[:, (2 * pr + 1) * pw:(2 * pr + 2) * pw] = jnp.where(lane < SB_HEAD_DIM, 0.0, kp).astype(k_ref.dtype)
        vp = vt[pr * pw:(pr + 1) * pw, :]
        v_ref[2 * pr * pw:(2 * pr + 1) * pw, :] = jnp.where(sub < SB_HEAD_DIM, vp, 0.0).astype(v_ref.dtype)
        v_ref[(2 * pr + 1) * pw:(2 * pr + 2) * pw, :] = jnp.where(sub < SB_HEAD_DIM, 0.0, vp).astype(v_ref.dtype)


def _odd_core(pq, pc, wg, ba, bx, lam, qn, kn, mavg):
    b, t, _ = pq.shape
    r = min(256, t)
    full = lambda a: pl.BlockSpec(a.shape, lambda bi, ti: (0,) * a.ndim)
    smalls = (wg, ba, bx, lam, qn, kn, mavg)
    blk = lambda w: pl.BlockSpec((None, r, w), lambda bi, ti: (bi, ti, 0))
    return pl.pallas_call(
        _odd_core_kernel,
        grid=(b, t // r),
        in_specs=[blk(3 * SB_WIDTH + LRU_WIDTH), blk(LRU_WIDTH)] + [full(a) for a in smalls],
        out_specs=[blk(LRU_WIDTH), blk(SB_WIDTH), blk(2 * SB_WIDTH),
                   pl.BlockSpec((None, 2 * SB_WIDTH, r), lambda bi, ti: (bi, 0, ti))],
        out_shape=[jax.ShapeDtypeStruct((b, t, LRU_WIDTH), BF16), jax.ShapeDtypeStruct((b, t, SB_WIDTH), BF16),
                   jax.ShapeDtypeStruct((b, t, 2 * SB_WIDTH), BF16), jax.ShapeDtypeStruct((b, 2 * SB_WIDTH, t), BF16)],
        scratch_shapes=[
            pltpu.VMEM((r, LRU_WIDTH), F32),
            pltpu.VMEM((r, LRU_WIDTH), F32),
            pltpu.VMEM((SUBLANES, LRU_WIDTH), F32),
        ],
        compiler_params=_params(("parallel", "arbitrary")),
        name="odd_core",
    )(pq, pc, *smalls)


SB_TQ = 512
SB_TK = 128
SB_UNROLL = 2
SB_STAGES = 5


def _sb_attn_kernel(q_ref, k_ref, vt_ref, ut_ref, o_ref, z_buf, lb_buf, l1b_buf, r0_buf, r0_prev, c_buf, sfx_buf, w_buf, acc):
    qi = pl.program_id(2)
    tq = q_ref.shape[0]
    tk = SB_TK
    pw = 2 * SB_HEAD_DIM
    ndiag = tq // tk
    jmax = qi * ndiag + (ndiag - 1)
    kloc = lax.broadcasted_iota(jnp.int32, (tk, tq), 0)
    qpos = qi * tq + lax.broadcasted_iota(jnp.int32, (tk, tq), 1)

    def key_start(t):
        return pl.multiple_of((jmax - t) * tk, tk)

    def p0(t):
        kx = k_ref[pl.ds(key_start(t), tk), :]
        km = jnp.concatenate([kx[:, :pw], kx[:, pw:]], axis=0)
        z_buf[...] = _dot_nt(km, q_ref[...])

    def p1(t, slot, masked):
        if masked:
            valid = (key_start(t) + kloc) < qpos
        for h in range(2):
            z = z_buf[h * tk:(h + 1) * tk, :]
            neg_abs = lax.bitcast_convert_type(lax.bitcast_convert_type(z, jnp.int32) | SIGN_BIT, F32)
            sp = jnp.log(1.0 + jnp.exp2(neg_abs)) * LOG2E
            lb = jnp.minimum(z, 0.0) - sp
            l1 = lb - z
            if masked:
                l1 = jnp.where(valid, l1, 0.0)
            lb_buf[slot, h * tk:(h + 1) * tk, :] = lb
            l1b_buf[:, h * tq:(h + 1) * tq] = l1.astype(BF16)
            r0_buf[:, h * tq:(h + 1) * tq] = l1[0:1, :]

    def p2():
        sfx_buf[...] = _dot(ut_ref[...], l1b_buf[...])
        r0_prev[...] = r0_buf[...]

    def p3(t, slot, masked):
        if masked:
            valid = (key_start(t) + kloc) < qpos
        carry = c_buf[...]
        for h in range(2):
            sfx = sfx_buf[:, h * tq:(h + 1) * tq] + carry[:, h * tq:(h + 1) * tq]
            w = jnp.exp2(lb_buf[slot, h * tk:(h + 1) * tk, :] + sfx)
            if masked:
                w = jnp.where(valid, w, 0.0)
            w_buf[h * tk:(h + 1) * tk, :] = w.astype(BF16)
        c_buf[...] = carry + sfx_buf[0:1, :] + r0_prev[...]

    def p4(t):
        vx = vt_ref[:, pl.ds(key_start(t), tk)]
        vbd = jnp.concatenate([vx[:pw], vx[pw:]], axis=1)
        acc[...] += _dot(vbd, w_buf[...])

    def emit(i, active, parity, masked):
        if 4 in active:
            p4(i - 4)
        if 3 in active:
            p3(i - 3, (parity - 3) % 2, masked(i - 3))
        if 2 in active:
            p2()
        if 1 in active:
            p1(i - 1, (parity - 1) % 2, masked(i - 1))
        if 0 in active:
            p0(i)

    def static_iterations(n_tiles, iters):
        for i in iters:
            active = {k for k in range(SB_STAGES) if 0 <= i - k < n_tiles}
            emit(i, active, i % 2, lambda t: t < ndiag)

    acc[...] = jnp.zeros_like(acc)
    c_buf[...] = jnp.zeros_like(c_buf)

    @pl.when(qi == 0)
    def _diagonal_only():
        static_iterations(ndiag, range(ndiag + SB_STAGES - 1))

    @pl.when(qi > 0)
    def _full():
        n = (qi + 1) * ndiag
        head = 2 * ndiag
        static_iterations(head, range(head))
        never = lambda t: False

        def body(jj, _):
            for u in range(SB_UNROLL):
                emit(head + SB_UNROLL * jj + u, set(range(SB_STAGES)), u % 2, never)
            return 0

        lax.fori_loop(0, (n - head) // SB_UNROLL, body, 0)
        for e in range(SB_STAGES - 1):
            emit(n + e, {k for k in range(SB_STAGES) if k > e}, e % 2, never)

    o_ref[...] = acc[...].T.astype(o_ref.dtype)


def _sb_attention(q, kx, vtx, ut):
    b, t, _ = q.shape
    tq = min(SB_TQ, t)
    tk = SB_TK
    pairs = SB_HEADS // 2
    pw = 2 * SB_HEAD_DIM
    return pl.pallas_call(
        _sb_attn_kernel,
        grid=(b, pairs, t // tq),
        in_specs=[
            pl.BlockSpec((None, tq, pw), lambda bi, pi, qi: (bi, qi, pi)),
            pl.BlockSpec((None, t, 2 * pw), lambda bi, pi, qi: (bi, 0, pi)),
            pl.BlockSpec((None, 2 * pw, t), lambda bi, pi, qi: (bi, pi, 0)),
            pl.BlockSpec(ut.shape, lambda bi, pi, qi: (0, 0)),
        ],
        out_specs=pl.BlockSpec((None, tq, pw), lambda bi, pi, qi: (bi, qi, pi)),
        out_shape=jax.ShapeDtypeStruct((b, t, SB_WIDTH), BF16),
        scratch_shapes=[
            pltpu.VMEM((2 * tk, tq), F32),
            pltpu.VMEM((2, 2 * tk, tq), F32),
            pltpu.VMEM((tk, 2 * tq), BF16),
            pltpu.VMEM((1, 2 * tq), F32),
            pltpu.VMEM((1, 2 * tq), F32),
            pltpu.VMEM((1, 2 * tq), F32),
            pltpu.VMEM((tk, 2 * tq), F32),
            pltpu.VMEM((2 * tk, tq), BF16),
            pltpu.VMEM((pw, tq), F32),
        ],
        compiler_params=_params(("parallel", "parallel", "arbitrary")),
        name="sb_attention",
    )(q, kx, vtx, ut)


def _const_tables():
    i = np.arange(128)
    tril = (i[None, :] <= i[:, None]).astype(np.float32)
    e_mat = np.zeros((128, SSD_D_INNER), np.float32)
    for h in range(SSD_HEADS):
        e_mat[h, h * SSD_HEADDIM:(h + 1) * SSD_HEADDIM] = 1.0
    uo = (i[None, :] > i[:, None]).astype(np.float32)
    hd = np.arange(SB_WIDTH) // SB_HEAD_DIM
    mavg = (hd[:, None] == hd[None, :]).astype(np.float32) / SB_HEAD_DIM
    return (jnp.asarray(tril, BF16), jnp.asarray(e_mat, BF16), jnp.asarray(uo, BF16), jnp.asarray(mavg, BF16))


def _row(v):
    return v.reshape(1, -1).astype(F32)


def _pad_lanes(v, width=128):
    return jnp.pad(v.reshape(1, -1).astype(F32), ((0, 0), (0, width - v.shape[-1])))


def kernel(x, norm_mix, norm_mlp, mlp_w1, mlp_w2, ev_w_in, ev_w_out, ret_qn, ret_kn, ret_gn, ssd_conv_w, ssd_conv_b, ssd_dt_bias, ssd_a_log, ssd_d, ssd_norm, od_w_in, od_w_out, lru_conv_w, lru_conv_b, lru_wa, lru_ba, lru_wx, lru_bx, lru_lam, sb_qn, sb_kn):
    b, t, d = x.shape
    n = b * t
    depth = norm_mix.shape[0]
    tril, e_mat, uo, mavg = _const_tables()
    cos, sin = _rope_tables(t)
    xf = x.reshape(n, d)
    for l in range(depth):
        g_mix = _row(norm_mix[l])
        if l % 2 == 0:
            e = l // 2
            w_in = jnp.pad(ev_w_in[e], ((0, 0), (0, EVEN_IN_PAD - EVEN_IN))).astype(BF16)
            plan = ((EV_Q, EV_G, "none", 0, 0), (EV_G, EV_XBC, "silu", 0, EV_G),
                    (EV_XBC, EV_DT, "conv_silu", 1, 0), (EV_DT, EVEN_IN_PAD, "none", 2, 0))
            p, px, dt = _norm_proj(xf, g_mix, w_in, ssd_conv_w[e].astype(F32), _row(ssd_conv_b[e]), plan,
                                   (EV_XBC, SSD_CONV_DIM, EVEN_IN_PAD - EV_DT), (BF16, BF16, F32), t)
            y = _even_core(
                p.reshape(b, t, EV_XBC), px.reshape(b, t, SSD_CONV_DIM), dt.reshape(b, t, EVEN_IN_PAD - EV_DT), cos, sin,
                _row(ret_qn[e]), _row(ret_kn[e]), _row(ret_gn[e]), _pad_lanes(ssd_dt_bias[e]), _pad_lanes(ssd_a_log[e]),
                _row(jnp.repeat(ssd_d[e], SSD_HEADDIM)), _row(ssd_norm[e]), e_mat, tril)
            xf = _outproj(xf, [y.reshape(n, EVEN_OUT)], [ev_w_out[e].astype(BF16)])
        else:
            o = l // 2
            w_in = od_w_in[o].astype(BF16)
            plan = ((OD_Q, ODD_IN, "none", 0, 0), (OD_GATE, OD_XC, "gelu", 0, ODD_IN - OD_Q), (OD_XC, OD_Q, "conv", 1, 0))
            pq, pc = _norm_proj(xf, g_mix, w_in, lru_conv_w[o].astype(F32), _row(lru_conv_b[o]), plan,
                                (ODD_IN - OD_Q + LRU_WIDTH, LRU_WIDTH), (BF16, BF16), t)
            wg = jnp.concatenate([lru_wa[o], lru_wx[o]], axis=-1).astype(BF16)
            yc, q, k, v = _odd_core(
                pq.reshape(b, t, 3 * SB_WIDTH + LRU_WIDTH), pc.reshape(b, t, LRU_WIDTH),
                wg, _row(lru_ba[o]), _row(lru_bx[o]),
                _row(lru_lam[o]), _row(jnp.tile(sb_qn[o], SB_HEADS)), _row(jnp.tile(sb_kn[o], SB_HEADS)), mavg)
            yd = _sb_attention(q, k, v, uo)
            w_out = od_w_out[o].astype(BF16)
            xf = _outproj(xf, [yc.reshape(n, LRU_WIDTH), yd.reshape(n, SB_WIDTH)], [w_out[:LRU_WIDTH], w_out[LRU_WIDTH:]])
        xf = _mlp(xf, _row(norm_mlp[l]), mlp_w1[l].astype(BF16), mlp_w2[l].astype(BF16))
    return xf.reshape(b, t, d)
```

```python
import functools
import math

import numpy as np
import jax
import jax.numpy as jnp
from jax import lax
from jax.experimental import pallas as pl
from jax.experimental.pallas import tpu as pltpu

F32 = jnp.float32
BF16 = jnp.bfloat16

D_MODEL = 1024
EPS = 1e-6

RET_HEADS = 4
RET_DK = 128
RET_DV = 256
ROPE_BASE = 10000.0
RET_LOG_GAMMA = tuple(float(np.log1p(-(2.0 ** (-5.0 - h)))) for h in range(RET_HEADS))

SSD_D_INNER = 1024
SSD_HEADDIM = 64
SSD_HEADS = 16
SSD_GROUPS = 2
SSD_STATE = 128
SSD_CONV = 4
SSD_CONV_DIM = SSD_D_INNER + 2 * SSD_GROUPS * SSD_STATE
CHUNK = 128
EVEN_ROWS = 512

LRU_WIDTH = 1024
LRU_BLOCKS = 8
LRU_BLOCK = 128
LRU_C = 8.0
LRU_CONV = 4

SB_HEADS = 8
SB_HEAD_DIM = 64
SB_WIDTH = SB_HEADS * SB_HEAD_DIM

D_FF = 4 * D_MODEL

EVEN_IN = 5648
EVEN_IN_PAD = 5760
EVEN_OUT = 2048
ODD_IN = 3584
ODD_OUT = 1536

EV_Q, EV_K, EV_V, EV_G, EV_Z, EV_XBC, EV_DT = 0, 512, 1024, 2048, 3072, 4096, 5632
OD_GATE, OD_XC, OD_Q, OD_K, OD_V = 0, 1024, 2048, 2560, 3072

LOG2E = 1.4426950408889634
SIGN_BIT = np.int32(-2 ** 31)
SUBLANES = 8
CONV_PAD = 8
VMEM_LIMIT = 56 * 1024 * 1024


def _params(sem):
    return pltpu.CompilerParams(dimension_semantics=sem, vmem_limit_bytes=VMEM_LIMIT)


def _dot(a, b):
    return jnp.dot(a, b, preferred_element_type=F32)


def _dot_nt(a, b):
    return lax.dot_general(a, b, (((1,), (1,)), ((), ())), preferred_element_type=F32)


def _dot_tn(a, b):
    return lax.dot_general(a, b, (((0,), (0,)), ((), ())), preferred_element_type=F32)


def _split3(x):
    hi = x.astype(BF16)
    r = x - hi.astype(F32)
    mid = r.astype(BF16)
    lo = (r - mid.astype(F32)).astype(BF16)
    return hi, mid, lo


def _dot_exact_lhs(x, m):
    hi, mid, lo = _split3(x)
    return _dot(hi, m) + _dot(mid, m) + _dot(lo, m)


def _dot_exact_rhs(m, x):
    hi, mid, lo = _split3(x)
    return _dot(m, hi) + _dot(m, mid) + _dot(m, lo)


def _sigmoid(x):
    return 1.0 / (1.0 + jnp.exp(-x))


def _silu(x):
    return x * _sigmoid(x)


def _softplus(x):
    return jnp.maximum(x, 0.0) + jnp.log(1.0 + jnp.exp(-jnp.abs(x)))


def _gelu_tanh(x):
    c = math.sqrt(2.0 / math.pi)
    return x * (0.5 * (1.0 + jnp.tanh(c * (x + 0.044715 * (x * x * x)))))


def _rms(x, g):
    return x * lax.rsqrt(jnp.mean(x * x, axis=-1, keepdims=True) + EPS) * g


def _causal_conv(acc, cw_ref, cb_ref, xpad, tail, reset):
    rows = acc.shape[0]
    taps = cw_ref.shape[0]
    xpad[0:CONV_PAD, :] = jnp.where(reset, 0.0, tail[...])
    xpad[CONV_PAD:CONV_PAD + rows, :] = acc
    y = cb_ref[...]
    for k in range(taps):
        off = CONV_PAD - (taps - 1) + k
        y = y + cw_ref[k:k + 1, :] * xpad[off:off + rows, :]
    tail[...] = xpad[rows:rows + CONV_PAD, :]
    return y


_ACTS = {"none": lambda v: v, "silu": _silu, "gelu": _gelu_tanh}
PROJ_TM = 512


def _norm_proj_kernel(plan, n_out, tiles_per_seq, x_ref, g_ref, w_ref, cw_ref, cb_ref, *rest):
    outs, (xpad, tail) = rest[:n_out], rest[n_out:]
    h = _rms(x_ref[...], g_ref[...]).astype(BF16)
    for lo, hi, op, out, dst in plan:
        acc = _dot(h, w_ref[:, lo:hi])
        if op.startswith("conv"):
            acc = _causal_conv(acc, cw_ref, cb_ref, xpad, tail, pl.program_id(0) % tiles_per_seq == 0)
            op = "silu" if op == "conv_silu" else "none"
        outs[out][:, dst:dst + hi - lo] = _ACTS[op](acc).astype(outs[out].dtype)


def _norm_proj(x, g, w, cw, cb, plan, out_widths, out_dtypes, seq_len):
    n, d = x.shape
    tm = min(PROJ_TM, n)
    row = lambda width: pl.BlockSpec((tm, width), lambda i: (i, 0))
    full = lambda a: pl.BlockSpec(a.shape, lambda i: (0, 0))
    resident = pl.BlockSpec(w.shape, lambda i: (0, 0), pipeline_mode=pl.Buffered(1))
    cwidth = cw.shape[1]
    return pl.pallas_call(
        functools.partial(_norm_proj_kernel, plan, len(out_widths), seq_len // tm),
        grid=(n // tm,),
        in_specs=[row(d), full(g), resident, full(cw), full(cb)],
        out_specs=[row(wd) for wd in out_widths],
        out_shape=[jax.ShapeDtypeStruct((n, wd), dt) for wd, dt in zip(out_widths, out_dtypes)],
        scratch_shapes=[pltpu.VMEM((tm + CONV_PAD, cwidth), F32), pltpu.VMEM((CONV_PAD, cwidth), F32)],
        compiler_params=_params(("arbitrary",)),
        name="norm_proj",
    )(x, g, w, cw, cb)


def _mlp_kernel(x_ref, g_ref, w1_ref, w2_ref, o_ref, h_scr, acc_scr):
    f = pl.program_id(1)

    @pl.when(f == 0)
    def _():
        h_scr[...] = _rms(x_ref[...], g_ref[...]).astype(BF16)
        acc_scr[...] = jnp.zeros_like(acc_scr)

    a = _dot(h_scr[...], w1_ref[...])
    a = jnp.square(jnp.maximum(a, 0.0)).astype(BF16)
    acc_scr[...] += _dot(a, w2_ref[...])

    @pl.when(f == pl.num_programs(1) - 1)
    def _():
        o_ref[...] = x_ref[...] + acc_scr[...]


def _mlp(x, g, w1, w2):
    n, d = x.shape
    dff = w1.shape[1]
    tm = min(1024, n)
    tf = 512
    return pl.pallas_call(
        _mlp_kernel,
        grid=(n // tm, dff // tf),
        in_specs=[
            pl.BlockSpec((tm, d), lambda i, f: (i, 0)),
            pl.BlockSpec((1, d), lambda i, f: (0, 0)),
            pl.BlockSpec((d, tf), lambda i, f: (0, f)),
            pl.BlockSpec((tf, d), lambda i, f: (f, 0)),
        ],
        out_specs=pl.BlockSpec((tm, d), lambda i, f: (i, 0)),
        out_shape=jax.ShapeDtypeStruct((n, d), F32),
        scratch_shapes=[pltpu.VMEM((tm, d), BF16), pltpu.VMEM((tm, d), F32)],
        compiler_params=_params(("parallel", "arbitrary")),
        name="mlp",
    )(x, g, w1, w2)


def _outproj1_kernel(x_ref, y_ref, w_ref, o_ref):
    o_ref[...] = x_ref[...] + _dot(y_ref[...], w_ref[...])


def _outproj2_kernel(x_ref, ya_ref, yb_ref, wa_ref, wb_ref, o_ref):
    o_ref[...] = x_ref[...] + _dot(ya_ref[...], wa_ref[...]) + _dot(yb_ref[...], wb_ref[...])


def _outproj(x, ys, ws):
    n, d = x.shape
    tm = min(512, n)
    kern = _outproj1_kernel if len(ys) == 1 else _outproj2_kernel
    in_specs = [pl.BlockSpec((tm, d), lambda i: (i, 0))]
    in_specs += [pl.BlockSpec((tm, y.shape[1]), lambda i: (i, 0)) for y in ys]
    in_specs += [pl.BlockSpec(w.shape, lambda i: (0, 0)) for w in ws]
    return pl.pallas_call(
        kern,
        grid=(n // tm,),
        in_specs=in_specs,
        out_specs=pl.BlockSpec((tm, d), lambda i: (i, 0)),
        out_shape=jax.ShapeDtypeStruct((n, d), F32),
        compiler_params=_params(("parallel",)),
        name="outproj",
    )(x, *ys, *ws)


def _rope_kernel(inv_ref, cos_ref, sin_ref):
    rows = cos_ref.shape[0]
    pos = (pl.program_id(0) * rows + lax.broadcasted_iota(jnp.int32, (rows, RET_DK), 0)).astype(F32)
    lane = lax.broadcasted_iota(jnp.int32, (rows, RET_DK), 1)
    ang = pos * inv_ref[...]
    cos_ref[...] = jnp.cos(ang)
    sin_ref[...] = jnp.where(lane < RET_DK // 2, -jnp.sin(ang), jnp.sin(ang))


def _rope_tables(t):
    half = RET_DK // 2
    inv = ROPE_BASE ** (-jnp.arange(half, dtype=F32) / half)
    inv = jnp.concatenate([inv, inv])[None, :]
    rows = min(512, t)
    return pl.pallas_call(
        _rope_kernel,
        grid=(t // rows,),
        in_specs=[pl.BlockSpec((1, RET_DK), lambda i: (0, 0))],
        out_specs=[pl.BlockSpec((rows, RET_DK), lambda i: (i, 0))] * 2,
        out_shape=[jax.ShapeDtypeStruct((t, RET_DK), F32)] * 2,
        compiler_params=_params(("parallel",)),
        name="rope_tables",
    )(inv)


def _even_core_kernel(p_ref, px_ref, dt_ref, cos_ref, sin_ref, qn_ref, kn_ref, gn_ref, dtb_ref, alog_ref,
                      dsk_ref, sg_ref, e_ref, tril_ref, y_ref, rstate, sstate, decay_scr):
    t = pl.program_id(1)
    c = CHUNK
    row = lax.broadcasted_iota(jnp.int32, (c, c), 0)
    col = lax.broadcasted_iota(jnp.int32, (c, c), 1)
    causal = row >= col

    @pl.when(t == 0)
    def _init():
        rstate[...] = jnp.zeros_like(rstate)
        sstate[...] = jnp.zeros_like(sstate)
        rel = (row - col).astype(F32)
        for h in range(RET_HEADS):
            decay_scr[h] = jnp.where(causal, jnp.exp(RET_LOG_GAMMA[h] * jnp.maximum(rel, 0.0)), 0.0)

    idx = lax.broadcasted_iota(jnp.int32, (c, 1), 0).astype(F32)
    for ci in range(p_ref.shape[0] // c):
        _even_chunk(slice(ci * c, (ci + 1) * c), idx, causal, p_ref, px_ref, dt_ref, cos_ref, sin_ref, qn_ref, kn_ref,
                    gn_ref, dtb_ref, alog_ref, dsk_ref, sg_ref, e_ref, tril_ref, y_ref, rstate, sstate, decay_scr)


def _even_chunk(rows, idx, causal, p_ref, px_ref, dt_ref, cos_ref, sin_ref, qn_ref, kn_ref, gn_ref, dtb_ref, alog_ref,
                dsk_ref, sg_ref, e_ref, tril_ref, y_ref, rstate, sstate, decay_scr):
    c = CHUNK
    cos = cos_ref[rows, :]
    sin = sin_ref[rows, :]
    for h in range(RET_HEADS):
        lg = RET_LOG_GAMMA[h]
        qh = _rms(p_ref[rows, EV_Q + h * RET_DK:EV_Q + (h + 1) * RET_DK].astype(F32), qn_ref[...])
        kh = _rms(p_ref[rows, EV_K + h * RET_DK:EV_K + (h + 1) * RET_DK].astype(F32), kn_ref[...])
        qh = qh * cos + pltpu.roll(qh, RET_DK // 2, 1) * sin
        kh = (kh * cos + pltpu.roll(kh, RET_DK // 2, 1) * sin) * (RET_DK ** -0.5)
        vb = p_ref[rows, EV_V + h * RET_DV:EV_V + (h + 1) * RET_DV]
        scores = _dot_nt(qh.astype(BF16), kh.astype(BF16)) * decay_scr[h]
        inner = _dot(scores.astype(BF16), vb)
        s_prev = rstate[h]
        q_dec = jnp.exp(lg * (idx + 1.0))
        cross = _dot((qh * q_dec).astype(BF16), s_prev.astype(BF16))
        k_dec = jnp.exp(lg * (float(c - 1) - idx))
        rstate[h] = s_prev * math.exp(lg * c) + _dot_tn((kh * k_dec).astype(BF16), vb)
        ya = inner + cross
        yc = ya - jnp.mean(ya, axis=-1, keepdims=True)
        yn = yc * lax.rsqrt(jnp.mean(yc * yc, axis=-1, keepdims=True) + EPS)
        yn = yn * gn_ref[:, h * RET_DV:(h + 1) * RET_DV]
        gate = p_ref[rows, EV_G + h * RET_DV:EV_G + (h + 1) * RET_DV].astype(F32)
        y_ref[rows, h * RET_DV:(h + 1) * RET_DV] = (gate * yn).astype(y_ref.dtype)

    xs = px_ref[rows, :SSD_D_INNER].astype(F32)
    gs = SSD_GROUPS * SSD_STATE
    bm = px_ref[rows, SSD_D_INNER:SSD_D_INNER + gs]
    cm = px_ref[rows, SSD_D_INNER + gs:SSD_D_INNER + 2 * gs]

    dt = _softplus(dt_ref[rows, :] + dtb_ref[...])
    a_neg = -jnp.exp(alog_ref[...])
    d_a = dt * a_neg
    acum = _dot_exact_rhs(tril_ref[...], d_a)
    acum_t = acum.T
    exp_a = jnp.exp(acum)
    dec = jnp.exp(acum[c - 1:c, :] - acum)
    e_mat = e_ref[...]
    dt_e = _dot_exact_lhs(dt, e_mat)
    exp_a_e = _dot_exact_lhs(exp_a, e_mat)
    dec_e = _dot_exact_lhs(dec, e_mat)
    xr = xs * dt_e
    xrd_b = (xr * dec_e).astype(BF16)
    cdec_e = exp_a_e[c - 1:c, :]
    lane = lax.broadcasted_iota(jnp.int32, (c, 128), 1)
    left = lane < SSD_HEADDIM
    hpg = SSD_HEADS // SSD_GROUPS
    gw = hpg * SSD_HEADDIM
    z_gate = p_ref[rows, EV_Z:EV_Z + SSD_D_INNER].astype(F32)
    for g in range(SSD_GROUPS):
        bg = bm[:, g * SSD_STATE:(g + 1) * SSD_STATE]
        cg = cm[:, g * SSD_STATE:(g + 1) * SSD_STATE]
        cb_mat = _dot_nt(cg, bg)
        s_prev = sstate[g]
        y_off = _dot(cg, s_prev.astype(BF16)) * exp_a_e[:, g * gw:(g + 1) * gw]
        pairs = []
        for pr in range(hpg // 2):
            h0 = g * hpg + 2 * pr
            ms = []
            for h in (h0, h0 + 1):
                seg = acum[:, h:h + 1] - acum_t[h:h + 1, :]
                l_mat = jnp.where(causal, jnp.exp(jnp.minimum(seg, 0.0)), 0.0)
                ms.append((cb_mat * l_mat).astype(BF16))
            xp = xr[:, h0 * SSD_HEADDIM:(h0 + 2) * SSD_HEADDIM]
            xbd = jnp.concatenate([jnp.where(left, xp, 0.0), jnp.where(left, 0.0, xp)], axis=0).astype(BF16)
            pairs.append(_dot(jnp.concatenate(ms, axis=1), xbd))
        y_diag = jnp.concatenate(pairs, axis=1)
        sstate[g] = s_prev * cdec_e[:, g * gw:(g + 1) * gw] + _dot_tn(bg, xrd_b[:, g * gw:(g + 1) * gw])
        yb = y_diag + y_off + xs[:, g * gw:(g + 1) * gw] * dsk_ref[:, g * gw:(g + 1) * gw]
        yb = yb * z_gate[:, g * gw:(g + 1) * gw]
        yb = _rms(yb, sg_ref[:, g * gw:(g + 1) * gw])
        lo = RET_HEADS * RET_DV + g * gw
        y_ref[rows, lo:lo + gw] = yb.astype(y_ref.dtype)


def _even_core(p, px, dt, cos, sin, qn, kn, gn, dtb, alog, dsk, sg, e_mat, tril):
    b, t, _ = p.shape
    c = min(EVEN_ROWS, t)
    full = lambda a: pl.BlockSpec(a.shape, lambda bi, ti: (0,) * a.ndim)
    smalls = (qn, kn, gn, dtb, alog, dsk, sg, e_mat, tril)
    return pl.pallas_call(
        _even_core_kernel,
        grid=(b, t // c),
        in_specs=[
            pl.BlockSpec((None, c, EV_XBC), lambda bi, ti: (bi, ti, 0)),
            pl.BlockSpec((None, c, SSD_CONV_DIM), lambda bi, ti: (bi, ti, 0)),
            pl.BlockSpec((None, c, EVEN_IN_PAD - EV_DT), lambda bi, ti: (bi, ti, 0)),
            pl.BlockSpec((c, RET_DK), lambda bi, ti: (ti, 0)),
            pl.BlockSpec((c, RET_DK), lambda bi, ti: (ti, 0)),
        ] + [full(a) for a in smalls],
        out_specs=pl.BlockSpec((None, c, EVEN_OUT), lambda bi, ti: (bi, ti, 0)),
        out_shape=jax.ShapeDtypeStruct((b, t, EVEN_OUT), BF16),
        scratch_shapes=[
            pltpu.VMEM((RET_HEADS, RET_DK, RET_DV), F32),
            pltpu.VMEM((SSD_GROUPS, SSD_STATE, SSD_D_INNER // SSD_GROUPS), F32),
            pltpu.VMEM((RET_HEADS, CHUNK, CHUNK), F32),
        ],
        compiler_params=_params(("parallel", "arbitrary")),
        name="even_core",
    )(p, px, dt, cos, sin, *smalls)


def _odd_core_kernel(pq_ref, pc_ref, wg_ref, ba_ref, bx_ref, lam_ref, qn_ref, kn_ref, mavg_ref,
                     yc_ref, q_ref, k_ref, v_ref, a_scr, b_scr, hcarry):
    t = pl.program_id(1)
    r = pc_ref.shape[0]

    @pl.when(t == 0)
    def _init():
        hcarry[...] = jnp.zeros_like(hcarry)

    xc = pc_ref[...].astype(F32)

    rs, is_ = [], []
    for blk in range(LRU_BLOCKS):
        g = _dot(xc[:, blk * LRU_BLOCK:(blk + 1) * LRU_BLOCK].astype(BF16), wg_ref[blk])
        rs.append(g[:, :LRU_BLOCK])
        is_.append(g[:, LRU_BLOCK:])
    rg = _sigmoid(jnp.concatenate(rs, axis=1) + ba_ref[...])
    ig = _sigmoid(jnp.concatenate(is_, axis=1) + bx_ref[...])
    log_a = -LRU_C * rg * _softplus(-lam_ref[...])
    a_scr[...] = jnp.exp(log_a)
    b_scr[...] = jnp.sqrt(1.0 - jnp.exp(2.0 * log_a)) * (ig * xc)

    rowi = lax.broadcasted_iota(jnp.int32, (SUBLANES, LRU_WIDTH), 0)

    def body(gi, carry):
        off = pl.multiple_of(gi * SUBLANES, SUBLANES)
        a = a_scr[pl.ds(off, SUBLANES), :]
        bv = b_scr[pl.ds(off, SUBLANES), :]
        for s in (1, 2, 4):
            a_sh = jnp.where(rowi >= s, pltpu.roll(a, s, 0), 1.0)
            b_sh = jnp.where(rowi >= s, pltpu.roll(bv, s, 0), 0.0)
            bv = a * b_sh + bv
            a = a * a_sh
        h = a * carry + bv
        b_scr[pl.ds(off, SUBLANES), :] = h
        return jnp.broadcast_to(h[SUBLANES - 1:SUBLANES, :], (SUBLANES, LRU_WIDTH))

    hcarry[...] = lax.fori_loop(0, r // SUBLANES, body, hcarry[...])
    yc_ref[...] = (b_scr[...] * pq_ref[:, 3 * SB_WIDTH:].astype(F32)).astype(yc_ref.dtype)

    def headnorm(x, g):
        sq = x * x
        hi = sq.astype(BF16)
        lo = (sq - hi.astype(F32)).astype(BF16)
        ms = _dot(hi, mavg_ref[...]) + _dot(lo, mavg_ref[...])
        return x * lax.rsqrt(ms + EPS) * g

    q = headnorm(pq_ref[:, 0:SB_WIDTH].astype(F32), qn_ref[...]) * (SB_HEAD_DIM ** -0.5 * LOG2E)
    q_ref[...] = q.astype(q_ref.dtype)
    kn = headnorm(pq_ref[:, SB_WIDTH:2 * SB_WIDTH].astype(F32), kn_ref[...])
    vt = pq_ref[:, 2 * SB_WIDTH:3 * SB_WIDTH].astype(F32).T
    pw = 2 * SB_HEAD_DIM
    lane = lax.broadcasted_iota(jnp.int32, (r, pw), 1)
    sub = lax.broadcasted_iota(jnp.int32, (pw, r), 0)
    for pr in range(SB_HEADS // 2):
        kp = kn[:, pr * pw:(pr + 1) * pw]
        k_ref[:, 2 * pr * pw:(2 * pr + 1) * pw] = jnp.where(lane < SB_HEAD_DIM, kp, 0.0).astype(k_ref.dtype)
        k_ref[:, (2 * pr + 1) * pw:(2 * pr + 2) * pw] = jnp.where(lane < SB_HEAD_DIM, 0.0, kp).astype(k_ref.dtype)
        vp = vt[pr * pw:(pr + 1) * pw, :]
        v_ref[2 * pr * pw:(2 * pr + 1) * pw, :] = jnp.where(sub < SB_HEAD_DIM, vp, 0.0).astype(v_ref.dtype)
        v_ref[(2 * pr + 1) * pw:(2 * pr + 2) * pw, :] = jnp.where(sub < SB_HEAD_DIM, 0.0, vp).astype(v_ref.dtype)


def _odd_core(pq, pc, wg, ba, bx, lam, qn, kn, mavg):
    b, t, _ = pq.shape
    r = min(512, t)
    full = lambda a: pl.BlockSpec(a.shape, lambda bi, ti: (0,) * a.ndim)
    smalls = (wg, ba, bx, lam, qn, kn, mavg)
    blk = lambda w: pl.BlockSpec((None, r, w), lambda bi, ti: (bi, ti, 0))
    return pl.pallas_call(
        _odd_core_kernel,
        grid=(b, t // r),
        in_specs=[blk(3 * SB_WIDTH + LRU_WIDTH), blk(LRU_WIDTH)] + [full(a) for a in smalls],
        out_specs=[blk(LRU_WIDTH), blk(SB_WIDTH), blk(2 * SB_WIDTH),
                   pl.BlockSpec((None, 2 * SB_WIDTH, r), lambda bi, ti: (bi, 0, ti))],
        out_shape=[jax.ShapeDtypeStruct((b, t, LRU_WIDTH), BF16), jax.ShapeDtypeStruct((b, t, SB_WIDTH), BF16),
                   jax.ShapeDtypeStruct((b, t, 2 * SB_WIDTH), BF16), jax.ShapeDtypeStruct((b, 2 * SB_WIDTH, t), BF16)],
        scratch_shapes=[
            pltpu.VMEM((r, LRU_WIDTH), F32),
            pltpu.VMEM((r, LRU_WIDTH), F32),
            pltpu.VMEM((SUBLANES, LRU_WIDTH), F32),
        ],
        compiler_params=_params(("parallel", "arbitrary")),
        name="odd_core",
    )(pq, pc, *smalls)


SB_TQ = 512
SB_TK = 128
SB_UNROLL = 2
SB_STAGES = 5


def _sb_attn_kernel(q_ref, k_ref, vt_ref, ut_ref, o_ref, z_buf, lb_buf, l1b_buf, r0_buf, r0_prev, c_buf, sfx_buf, w_buf, acc):
    qi = pl.program_id(2)
    tq = q_ref.shape[0]
    tk = SB_TK
    pw = 2 * SB_HEAD_DIM
    ndiag = tq // tk
    jmax = qi * ndiag + (ndiag - 1)
    kloc = lax.broadcasted_iota(jnp.int32, (tk, tq), 0)
    qpos = qi * tq + lax.broadcasted_iota(jnp.int32, (tk, tq), 1)

    def key_start(t):
        return pl.multiple_of((jmax - t) * tk, tk)

    def p0(t):
        kx = k_ref[pl.ds(key_start(t), tk), :]
        km = jnp.concatenate([kx[:, :pw], kx[:, pw:]], axis=0)
        z_buf[...] = _dot_nt(km, q_ref[...])

    def p1(t, slot, masked):
        if masked:
            valid = (key_start(t) + kloc) < qpos
        for h in range(2):
            z = z_buf[h * tk:(h + 1) * tk, :]
            neg_abs = lax.bitcast_convert_type(lax.bitcast_convert_type(z, jnp.int32) | SIGN_BIT, F32)
            sp = jnp.log(1.0 + jnp.exp2(neg_abs)) * LOG2E
            lb = jnp.minimum(z, 0.0) - sp
            l1 = lb - z
            if masked:
                l1 = jnp.where(valid, l1, 0.0)
            lb_buf[slot, h * tk:(h + 1) * tk, :] = lb
            l1b_buf[:, h * tq:(h + 1) * tq] = l1.astype(BF16)
            r0_buf[:, h * tq:(h + 1) * tq] = l1[0:1, :]

    def p2():
        sfx_buf[...] = _dot(ut_ref[...], l1b_buf[...])
        r0_prev[...] = r0_buf[...]

    def p3(t, slot, masked):
        if masked:
            valid = (key_start(t) + kloc) < qpos
        carry = c_buf[...]
        for h in range(2):
            sfx = sfx_buf[:, h * tq:(h + 1) * tq] + carry[:, h * tq:(h + 1) * tq]
            w = jnp.exp2(lb_buf[slot, h * tk:(h + 1) * tk, :] + sfx)
            if masked:
                w = jnp.where(valid, w, 0.0)
            w_buf[h * tk:(h + 1) * tk, :] = w.astype(BF16)
        c_buf[...] = carry + sfx_buf[0:1, :] + r0_prev[...]

    def p4(t):
        vx = vt_ref[:, pl.ds(key_start(t), tk)]
        vbd = jnp.concatenate([vx[:pw], vx[pw:]], axis=1)
        acc[...] += _dot(vbd, w_buf[...])

    def emit(i, active, parity, masked):
        if 4 in active:
            p4(i - 4)
        if 3 in active:
            p3(i - 3, (parity - 3) % 2, masked(i - 3))
        if 2 in active:
            p2()
        if 1 in active:
            p1(i - 1, (parity - 1) % 2, masked(i - 1))
        if 0 in active:
            p0(i)

    def static_iterations(n_tiles, iters):
        for i in iters:
            active = {k for k in range(SB_STAGES) if 0 <= i - k < n_tiles}
            emit(i, active, i % 2, lambda t: t < ndiag)

    acc[...] = jnp.zeros_like(acc)
    c_buf[...] = jnp.zeros_like(c_buf)

    @pl.when(qi == 0)
    def _diagonal_only():
        static_iterations(ndiag, range(ndiag + SB_STAGES - 1))

    @pl.when(qi > 0)
    def _full():
        n = (qi + 1) * ndiag
        head = 2 * ndiag
        static_iterations(head, range(head))
        never = lambda t: False

        def body(jj, _):
            for u in range(SB_UNROLL):
                emit(head + SB_UNROLL * jj + u, set(range(SB_STAGES)), u % 2, never)
            return 0

        lax.fori_loop(0, (n - head) // SB_UNROLL, body, 0)
        for e in range(SB_STAGES - 1):
            emit(n + e, {k for k in range(SB_STAGES) if k > e}, e % 2, never)

    o_ref[...] = acc[...].T.astype(o_ref.dtype)


def _sb_attention(q, kx, vtx, ut):
    b, t, _ = q.shape
    tq = min(SB_TQ, t)
    tk = SB_TK
    pairs = SB_HEADS // 2
    pw = 2 * SB_HEAD_DIM
    return pl.pallas_call(
        _sb_attn_kernel,
        grid=(b, pairs, t // tq),
        in_specs=[
            pl.BlockSpec((None, tq, pw), lambda bi, pi, qi: (bi, qi, pi)),
            pl.BlockSpec((None, t, 2 * pw), lambda bi, pi, qi: (bi, 0, pi)),
            pl.BlockSpec((None, 2 * pw, t), lambda bi, pi, qi: (bi, pi, 0)),
            pl.BlockSpec(ut.shape, lambda bi, pi, qi: (0, 0)),
        ],
        out_specs=pl.BlockSpec((None, tq, pw), lambda bi, pi, qi: (bi, qi, pi)),
        out_shape=jax.ShapeDtypeStruct((b, t, SB_WIDTH), BF16),
        scratch_shapes=[
            pltpu.VMEM((2 * tk, tq), F32),
            pltpu.VMEM((2, 2 * tk, tq), F32),
            pltpu.VMEM((tk, 2 * tq), BF16),
            pltpu.VMEM((1, 2 * tq), F32),
            pltpu.VMEM((1, 2 * tq), F32),
            pltpu.VMEM((1, 2 * tq), F32),
            pltpu.VMEM((tk, 2 * tq), F32),
            pltpu.VMEM((2 * tk, tq), BF16),
            pltpu.VMEM((pw, tq), F32),
        ],
        compiler_params=_params(("parallel", "parallel", "arbitrary")),
        name="sb_attention",
    )(q, kx, vtx, ut)


def _const_tables():
    i = np.arange(128)
    tril = (i[None, :] <= i[:, None]).astype(np.float32)
    e_mat = np.zeros((128, SSD_D_INNER), np.float32)
    for h in range(SSD_HEADS):
        e_mat[h, h * SSD_HEADDIM:(h + 1) * SSD_HEADDIM] = 1.0
    uo = (i[None, :] > i[:, None]).astype(np.float32)
    hd = np.arange(SB_WIDTH) // SB_HEAD_DIM
    mavg = (hd[:, None] == hd[None, :]).astype(np.float32) / SB_HEAD_DIM
    return (jnp.asarray(tril, BF16), jnp.asarray(e_mat, BF16), jnp.asarray(uo, BF16), jnp.asarray(mavg, BF16))


def _row(v):
    return v.reshape(1, -1).astype(F32)


def _pad_lanes(v, width=128):
    return jnp.pad(v.reshape(1, -1).astype(F32), ((0, 0), (0, width - v.shape[-1])))


def kernel(x, norm_mix, norm_mlp, mlp_w1, mlp_w2, ev_w_in, ev_w_out, ret_qn, ret_kn, ret_gn, ssd_conv_w, ssd_conv_b, ssd_dt_bias, ssd_a_log, ssd_d, ssd_norm, od_w_in, od_w_out, lru_conv_w, lru_conv_b, lru_wa, lru_ba, lru_wx, lru_bx, lru_lam, sb_qn, sb_kn):
    b, t, d = x.shape
    n = b * t
    depth = norm_mix.shape[0]
    tril, e_mat, uo, mavg = _const_tables()
    cos, sin = _rope_tables(t)
    xf = x.reshape(n, d)
    for l in range(depth):
        g_mix = _row(norm_mix[l])
        if l % 2 == 0:
            e = l // 2
            w_in = jnp.pad(ev_w_in[e], ((0, 0), (0, EVEN_IN_PAD - EVEN_IN))).astype(BF16)
            plan = ((EV_XBC, EV_DT, "conv_silu", 1, 0), (EV_G, EV_XBC, "silu", 0, EV_G),
                    (EV_DT, EVEN_IN_PAD, "none", 2, 0), (EV_Q, EV_G, "none", 0, 0))
            p, px, dt = _norm_proj(xf, g_mix, w_in, ssd_conv_w[e].astype(F32), _row(ssd_conv_b[e]), plan,
                                   (EV_XBC, SSD_CONV_DIM, EVEN_IN_PAD - EV_DT), (BF16, BF16, F32), t)
            y = _even_core(
                p.reshape(b, t, EV_XBC), px.reshape(b, t, SSD_CONV_DIM), dt.reshape(b, t, EVEN_IN_PAD - EV_DT), cos, sin,
                _row(ret_qn[e]), _row(ret_kn[e]), _row(ret_gn[e]), _pad_lanes(ssd_dt_bias[e]), _pad_lanes(ssd_a_log[e]),
                _row(jnp.repeat(ssd_d[e], SSD_HEADDIM)), _row(ssd_norm[e]), e_mat, tril)
            xf = _outproj(xf, [y.reshape(n, EVEN_OUT)], [ev_w_out[e].astype(BF16)])
        else:
            o = l // 2
            w_in = od_w_in[o].astype(BF16)
            plan = ((OD_XC, OD_Q, "conv", 1, 0), (OD_GATE, OD_XC, "gelu", 0, ODD_IN - OD_Q), (OD_Q, ODD_IN, "none", 0, 0))
            pq, pc = _norm_proj(xf, g_mix, w_in, lru_conv_w[o].astype(F32), _row(lru_conv_b[o]), plan,
                                (ODD_IN - OD_Q + LRU_WIDTH, LRU_WIDTH), (BF16, BF16), t)
            wg = jnp.concatenate([lru_wa[o], lru_wx[o]], axis=-1).astype(BF16)
            yc, q, k, v = _odd_core(
                pq.reshape(b, t, 3 * SB_WIDTH + LRU_WIDTH), pc.reshape(b, t, LRU_WIDTH),
                wg, _row(lru_ba[o]), _row(lru_bx[o]),
                _row(lru_lam[o]), _row(jnp.tile(sb_qn[o], SB_HEADS)), _row(jnp.tile(sb_kn[o], SB_HEADS)), mavg)
            yd = _sb_attention(q, k, v, uo)
            w_out = od_w_out[o].astype(BF16)
            xf = _outproj(xf, [yc.reshape(n, LRU_WIDTH), yd.reshape(n, SB_WIDTH)], [w_out[:LRU_WIDTH], w_out[LRU_WIDTH:]])
        xf = _mlp(xf, _row(norm_mlp[l]), mlp_w1[l].astype(BF16), mlp_w2[l].astype(BF16))
    return xf.reshape(b, t, d)
```

```python
import functools
import math

import numpy as np
import jax
import jax.numpy as jnp
from jax import lax
from jax.experimental import pallas as pl
from jax.experimental.pallas import tpu as pltpu

F32 = jnp.float32
BF16 = jnp.bfloat16

D_MODEL = 1024
EPS = 1e-6

RET_HEADS = 4
RET_DK = 128
RET_DV = 256
ROPE_BASE = 10000.0
RET_LOG_GAMMA = tuple(float(np.log1p(-(2.0 ** (-5.0 - h)))) for h in range(RET_HEADS))

SSD_D_INNER = 1024
SSD_HEADDIM = 64
SSD_HEADS = 16
SSD_GROUPS = 2
SSD_STATE = 128
SSD_CONV = 4
SSD_CONV_DIM = SSD_D_INNER + 2 * SSD_GROUPS * SSD_STATE
CHUNK = 128
EVEN_ROWS = 512

LRU_WIDTH = 1024
LRU_BLOCKS = 8
LRU_BLOCK = 128
LRU_C = 8.0
LRU_CONV = 4

SB_HEADS = 8
SB_HEAD_DIM = 64
SB_WIDTH = SB_HEADS * SB_HEAD_DIM

D_FF = 4 * D_MODEL

EVEN_IN = 5648
EVEN_IN_PAD = 5760
EVEN_OUT = 2048
ODD_IN = 3584
ODD_OUT = 1536

EV_Q, EV_K, EV_V, EV_G, EV_Z, EV_XBC, EV_DT = 0, 512, 1024, 2048, 3072, 4096, 5632
OD_GATE, OD_XC, OD_Q, OD_K, OD_V = 0, 1024, 2048, 2560, 3072

LOG2E = 1.4426950408889634
SIGN_BIT = np.int32(-2 ** 31)
SUBLANES = 8
CONV_PAD = 8
VMEM_LIMIT = 56 * 1024 * 1024


def _params(sem):
    return pltpu.CompilerParams(dimension_semantics=sem, vmem_limit_bytes=VMEM_LIMIT)


def _dot(a, b):
    return jnp.dot(a, b, preferred_element_type=F32)


def _dot_nt(a, b):
    return lax.dot_general(a, b, (((1,), (1,)), ((), ())), preferred_element_type=F32)


def _dot_tn(a, b):
    return lax.dot_general(a, b, (((0,), (0,)), ((), ())), preferred_element_type=F32)


def _split3(x):
    hi = x.astype(BF16)
    r = x - hi.astype(F32)
    mid = r.astype(BF16)
    lo = (r - mid.astype(F32)).astype(BF16)
    return hi, mid, lo


def _dot_exact_lhs(x, m):
    hi, mid, lo = _split3(x)
    return _dot(hi, m) + _dot(mid, m) + _dot(lo, m)


def _dot_exact_rhs(m, x):
    hi, mid, lo = _split3(x)
    return _dot(m, hi) + _dot(m, mid) + _dot(m, lo)


def _sigmoid(x):
    return 1.0 / (1.0 + jnp.exp(-x))


def _silu(x):
    return x * _sigmoid(x)


def _softplus(x):
    return jnp.maximum(x, 0.0) + jnp.log(1.0 + jnp.exp(-jnp.abs(x)))


def _gelu_tanh(x):
    c = math.sqrt(2.0 / math.pi)
    return x * (0.5 * (1.0 + jnp.tanh(c * (x + 0.044715 * (x * x * x)))))


def _rms(x, g):
    return x * lax.rsqrt(jnp.mean(x * x, axis=-1, keepdims=True) + EPS) * g


def _causal_conv(acc, cw_ref, cb_ref, xpad, tail, reset):
    rows = acc.shape[0]
    taps = cw_ref.shape[0]
    xpad[0:CONV_PAD, :] = jnp.where(reset, 0.0, tail[...])
    xpad[CONV_PAD:CONV_PAD + rows, :] = acc
    y = cb_ref[...]
    for k in range(taps):
        off = CONV_PAD - (taps - 1) + k
        y = y + cw_ref[k:k + 1, :] * xpad[off:off + rows, :]
    tail[...] = xpad[rows:rows + CONV_PAD, :]
    return y


_ACTS = {"none": lambda v: v, "silu": _silu, "gelu": _gelu_tanh}
PROJ_TM = 512


def _norm_proj_kernel(plan, n_out, tiles_per_seq, x_ref, g_ref, w_ref, cw_ref, cb_ref, *rest):
    outs, (xpad, tail) = rest[:n_out], rest[n_out:]
    h = _rms(x_ref[...], g_ref[...]).astype(BF16)
    for lo, hi, op, out, dst in plan:
        acc = _dot(h, w_ref[:, lo:hi])
        if op.startswith("conv"):
            acc = _causal_conv(acc, cw_ref, cb_ref, xpad, tail, pl.program_id(0) % tiles_per_seq == 0)
            op = "silu" if op == "conv_silu" else "none"
        outs[out][:, dst:dst + hi - lo] = _ACTS[op](acc).astype(outs[out].dtype)


def _norm_proj(x, g, w, cw, cb, plan, out_widths, out_dtypes, seq_len):
    n, d = x.shape
    tm = min(PROJ_TM, n)
    row = lambda width: pl.BlockSpec((tm, width), lambda i: (i, 0))
    full = lambda a: pl.BlockSpec(a.shape, lambda i: (0, 0))
    resident = pl.BlockSpec(w.shape, lambda i: (0, 0), pipeline_mode=pl.Buffered(1))
    cwidth = cw.shape[1]
    return pl.pallas_call(
        functools.partial(_norm_proj_kernel, plan, len(out_widths), seq_len // tm),
        grid=(n // tm,),
        in_specs=[row(d), full(g), resident, full(cw), full(cb)],
        out_specs=[row(wd) for wd in out_widths],
        out_shape=[jax.ShapeDtypeStruct((n, wd), dt) for wd, dt in zip(out_widths, out_dtypes)],
        scratch_shapes=[pltpu.VMEM((tm + CONV_PAD, cwidth), F32), pltpu.VMEM((CONV_PAD, cwidth), F32)],
        compiler_params=_params(("arbitrary",)),
        name="norm_proj",
    )(x, g, w, cw, cb)


def _mlp_kernel(x_ref, g_ref, w1_ref, w2_ref, o_ref, h_scr, acc_scr):
    f = pl.program_id(1)

    @pl.when(f == 0)
    def _():
        h_scr[...] = _rms(x_ref[...], g_ref[...]).astype(BF16)
        acc_scr[...] = jnp.zeros_like(acc_scr)

    a = _dot(h_scr[...], w1_ref[...])
    a = jnp.square(jnp.maximum(a, 0.0)).astype(BF16)
    acc_scr[...] += _dot(a, w2_ref[...])

    @pl.when(f == pl.num_programs(1) - 1)
    def _():
        o_ref[...] = x_ref[...] + acc_scr[...]


def _mlp(x, g, w1, w2):
    n, d = x.shape
    dff = w1.shape[1]
    tm = min(1024, n)
    tf = 512
    return pl.pallas_call(
        _mlp_kernel,
        grid=(n // tm, dff // tf),
        in_specs=[
            pl.BlockSpec((tm, d), lambda i, f: (i, 0)),
            pl.BlockSpec((1, d), lambda i, f: (0, 0)),
            pl.BlockSpec((d, tf), lambda i, f: (0, f)),
            pl.BlockSpec((tf, d), lambda i, f: (f, 0)),
        ],
        out_specs=pl.BlockSpec((tm, d), lambda i, f: (i, 0)),
        out_shape=jax.ShapeDtypeStruct((n, d), F32),
        scratch_shapes=[pltpu.VMEM((tm, d), BF16), pltpu.VMEM((tm, d), F32)],
        compiler_params=_params(("parallel", "arbitrary")),
        name="mlp",
    )(x, g, w1, w2)


def _outproj1_kernel(x_ref, y_ref, w_ref, o_ref):
    o_ref[...] = x_ref[...] + _dot(y_ref[...], w_ref[...])


def _outproj2_kernel(x_ref, ya_ref, yb_ref, wa_ref, wb_ref, o_ref):
    o_ref[...] = x_ref[...] + _dot(ya_ref[...], wa_ref[...]) + _dot(yb_ref[...], wb_ref[...])


def _outproj(x, ys, ws):
    n, d = x.shape
    tm = min(512, n)
    kern = _outproj1_kernel if len(ys) == 1 else _outproj2_kernel
    in_specs = [pl.BlockSpec((tm, d), lambda i: (i, 0))]
    in_specs += [pl.BlockSpec((tm, y.shape[1]), lambda i: (i, 0)) for y in ys]
    in_specs += [pl.BlockSpec(w.shape, lambda i: (0, 0)) for w in ws]
    return pl.pallas_call(
        kern,
        grid=(n // tm,),
        in_specs=in_specs,
        out_specs=pl.BlockSpec((tm, d), lambda i: (i, 0)),
        out_shape=jax.ShapeDtypeStruct((n, d), F32),
        compiler_params=_params(("parallel",)),
        name="outproj",
    )(x, *ys, *ws)


def _rope_kernel(inv_ref, cos_ref, sin_ref):
    rows = cos_ref.shape[0]
    pos = (pl.program_id(0) * rows + lax.broadcasted_iota(jnp.int32, (rows, RET_DK), 0)).astype(F32)
    lane = lax.broadcasted_iota(jnp.int32, (rows, RET_DK), 1)
    ang = pos * inv_ref[...]
    cos_ref[...] = jnp.cos(ang)
    sin_ref[...] = jnp.where(lane < RET_DK // 2, -jnp.sin(ang), jnp.sin(ang))


def _rope_tables(t):
    half = RET_DK // 2
    inv = ROPE_BASE ** (-jnp.arange(half, dtype=F32) / half)
    inv = jnp.concatenate([inv, inv])[None, :]
    rows = min(512, t)
    return pl.pallas_call(
        _rope_kernel,
        grid=(t // rows,),
        in_specs=[pl.BlockSpec((1, RET_DK), lambda i: (0, 0))],
        out_specs=[pl.BlockSpec((rows, RET_DK), lambda i: (i, 0))] * 2,
        out_shape=[jax.ShapeDtypeStruct((t, RET_DK), F32)] * 2,
        compiler_params=_params(("parallel",)),
        name="rope_tables",
    )(inv)


def _even_core_kernel(p_ref, px_ref, dt_ref, x_ref, cos_ref, sin_ref, qn_ref, kn_ref, gn_ref, dtb_ref, alog_ref,
                      dsk_ref, sg_ref, e_ref, tril_ref, wout_ref, o_ref, rstate, sstate, decay_scr, y_scr):
    t = pl.program_id(1)
    c = CHUNK
    row = lax.broadcasted_iota(jnp.int32, (c, c), 0)
    col = lax.broadcasted_iota(jnp.int32, (c, c), 1)
    causal = row >= col

    @pl.when(t == 0)
    def _init():
        rstate[...] = jnp.zeros_like(rstate)
        sstate[...] = jnp.zeros_like(sstate)
        y_scr[...] = jnp.zeros_like(y_scr)
        rel = (row - col).astype(F32)
        for h in range(RET_HEADS):
            decay_scr[h] = jnp.where(causal, jnp.exp(RET_LOG_GAMMA[h] * jnp.maximum(rel, 0.0)), 0.0)

    slot = t % 2
    o_ref[...] = x_ref[...] + _dot(y_scr[1 - slot], wout_ref[...])
    y_ref = y_scr.at[slot]
    idx = lax.broadcasted_iota(jnp.int32, (c, 1), 0).astype(F32)
    for ci in range(p_ref.shape[0] // c):
        _even_chunk(slice(ci * c, (ci + 1) * c), idx, causal, p_ref, px_ref, dt_ref, cos_ref, sin_ref, qn_ref, kn_ref,
                    gn_ref, dtb_ref, alog_ref, dsk_ref, sg_ref, e_ref, tril_ref, y_ref, rstate, sstate, decay_scr)


def _even_chunk(rows, idx, causal, p_ref, px_ref, dt_ref, cos_ref, sin_ref, qn_ref, kn_ref, gn_ref, dtb_ref, alog_ref,
                dsk_ref, sg_ref, e_ref, tril_ref, y_ref, rstate, sstate, decay_scr):
    c = CHUNK
    cos = cos_ref[rows, :]
    sin = sin_ref[rows, :]
    for h in range(RET_HEADS):
        lg = RET_LOG_GAMMA[h]
        qh = _rms(p_ref[rows, EV_Q + h * RET_DK:EV_Q + (h + 1) * RET_DK].astype(F32), qn_ref[...])
        kh = _rms(p_ref[rows, EV_K + h * RET_DK:EV_K + (h + 1) * RET_DK].astype(F32), kn_ref[...])
        qh = qh * cos + pltpu.roll(qh, RET_DK // 2, 1) * sin
        kh = (kh * cos + pltpu.roll(kh, RET_DK // 2, 1) * sin) * (RET_DK ** -0.5)
        vb = p_ref[rows, EV_V + h * RET_DV:EV_V + (h + 1) * RET_DV]
        scores = _dot_nt(qh.astype(BF16), kh.astype(BF16)) * decay_scr[h]
        inner = _dot(scores.astype(BF16), vb)
        s_prev = rstate[h]
        q_dec = jnp.exp(lg * (idx + 1.0))
        cross = _dot((qh * q_dec).astype(BF16), s_prev.astype(BF16))
        k_dec = jnp.exp(lg * (float(c - 1) - idx))
        rstate[h] = s_prev * math.exp(lg * c) + _dot_tn((kh * k_dec).astype(BF16), vb)
        ya = inner + cross
        yc = ya - jnp.mean(ya, axis=-1, keepdims=True)
        yn = yc * lax.rsqrt(jnp.mean(yc * yc, axis=-1, keepdims=True) + EPS)
        yn = yn * gn_ref[:, h * RET_DV:(h + 1) * RET_DV]
        gate = p_ref[rows, EV_G + h * RET_DV:EV_G + (h + 1) * RET_DV].astype(F32)
        y_ref[rows, h * RET_DV:(h + 1) * RET_DV] = (gate * yn).astype(y_ref.dtype)

    xs = px_ref[rows, :SSD_D_INNER].astype(F32)
    gs = SSD_GROUPS * SSD_STATE
    bm = px_ref[rows, SSD_D_INNER:SSD_D_INNER + gs]
    cm = px_ref[rows, SSD_D_INNER + gs:SSD_D_INNER + 2 * gs]

    dt = _softplus(dt_ref[rows, :] + dtb_ref[...])
    a_neg = -jnp.exp(alog_ref[...])
    d_a = dt * a_neg
    acum = _dot_exact_rhs(tril_ref[...], d_a)
    acum_t = acum.T
    exp_a = jnp.exp(acum)
    dec = jnp.exp(acum[c - 1:c, :] - acum)
    e_mat = e_ref[...]
    dt_e = _dot_exact_lhs(dt, e_mat)
    exp_a_e = _dot_exact_lhs(exp_a, e_mat)
    dec_e = _dot_exact_lhs(dec, e_mat)
    xr = xs * dt_e
    xrd_b = (xr * dec_e).astype(BF16)
    cdec_e = exp_a_e[c - 1:c, :]
    lane = lax.broadcasted_iota(jnp.int32, (c, 128), 1)
    left = lane < SSD_HEADDIM
    hpg = SSD_HEADS // SSD_GROUPS
    gw = hpg * SSD_HEADDIM
    z_gate = p_ref[rows, EV_Z:EV_Z + SSD_D_INNER].astype(F32)
    for g in range(SSD_GROUPS):
        bg = bm[:, g * SSD_STATE:(g + 1) * SSD_STATE]
        cg = cm[:, g * SSD_STATE:(g + 1) * SSD_STATE]
        cb_mat = _dot_nt(cg, bg)
        s_prev = sstate[g]
        y_off = _dot(cg, s_prev.astype(BF16)) * exp_a_e[:, g * gw:(g + 1) * gw]
        pairs = []
        for pr in range(hpg // 2):
            h0 = g * hpg + 2 * pr
            ms = []
            for h in (h0, h0 + 1):
                seg = acum[:, h:h + 1] - acum_t[h:h + 1, :]
                l_mat = jnp.where(causal, jnp.exp(jnp.minimum(seg, 0.0)), 0.0)
                ms.append((cb_mat * l_mat).astype(BF16))
            xp = xr[:, h0 * SSD_HEADDIM:(h0 + 2) * SSD_HEADDIM]
            xbd = jnp.concatenate([jnp.where(left, xp, 0.0), jnp.where(left, 0.0, xp)], axis=0).astype(BF16)
            pairs.append(_dot(jnp.concatenate(ms, axis=1), xbd))
        y_diag = jnp.concatenate(pairs, axis=1)
        sstate[g] = s_prev * cdec_e[:, g * gw:(g + 1) * gw] + _dot_tn(bg, xrd_b[:, g * gw:(g + 1) * gw])
        yb = y_diag + y_off + xs[:, g * gw:(g + 1) * gw] * dsk_ref[:, g * gw:(g + 1) * gw]
        yb = yb * z_gate[:, g * gw:(g + 1) * gw]
        yb = _rms(yb, sg_ref[:, g * gw:(g + 1) * gw])
        lo = RET_HEADS * RET_DV + g * gw
        y_ref[rows, lo:lo + gw] = yb.astype(y_ref.dtype)


def _even_core(p, px, dt, x, cos, sin, qn, kn, gn, dtb, alog, dsk, sg, e_mat, tril, w_out):
    b, t, _ = p.shape
    c = min(EVEN_ROWS, t)
    last = t // c - 1
    cur = lambda bi, ti: (bi, jnp.minimum(ti, last), 0)
    prev = lambda bi, ti: (bi, jnp.maximum(ti - 1, 0), 0)
    full = lambda a: pl.BlockSpec(a.shape, lambda bi, ti: (0,) * a.ndim)
    smalls = (qn, kn, gn, dtb, alog, dsk, sg, e_mat, tril, w_out)
    return pl.pallas_call(
        _even_core_kernel,
        grid=(b, t // c + 1),
        in_specs=[
            pl.BlockSpec((None, c, EV_XBC), cur),
            pl.BlockSpec((None, c, SSD_CONV_DIM), cur),
            pl.BlockSpec((None, c, EVEN_IN_PAD - EV_DT), cur),
            pl.BlockSpec((None, c, D_MODEL), prev),
            pl.BlockSpec((c, RET_DK), lambda bi, ti: (jnp.minimum(ti, last), 0)),
            pl.BlockSpec((c, RET_DK), lambda bi, ti: (jnp.minimum(ti, last), 0)),
        ] + [full(a) for a in smalls],
        out_specs=pl.BlockSpec((None, c, D_MODEL), prev),
        out_shape=jax.ShapeDtypeStruct((b, t, D_MODEL), F32),
        scratch_shapes=[
            pltpu.VMEM((RET_HEADS, RET_DK, RET_DV), F32),
            pltpu.VMEM((SSD_GROUPS, SSD_STATE, SSD_D_INNER // SSD_GROUPS), F32),
            pltpu.VMEM((RET_HEADS, CHUNK, CHUNK), F32),
            pltpu.VMEM((2, c, EVEN_OUT), BF16),
        ],
        compiler_params=_params(("parallel", "arbitrary")),
        name="even_core",
    )(p, px, dt, x, cos, sin, *smalls)


def _odd_core_kernel(pq_ref, pc_ref, wg_ref, ba_ref, bx_ref, lam_ref, qn_ref, kn_ref, mavg_ref,
                     yc_ref, q_ref, k_ref, v_ref, a_scr, b_scr, hcarry):
    t = pl.program_id(1)
    r = pc_ref.shape[0]

    @pl.when(t == 0)
    def _init():
        hcarry[...] = jnp.zeros_like(hcarry)

    xc = pc_ref[...].astype(F32)

    rs, is_ = [], []
    for blk in range(LRU_BLOCKS):
        g = _dot(xc[:, blk * LRU_BLOCK:(blk + 1) * LRU_BLOCK].astype(BF16), wg_ref[blk])
        rs.append(g[:, :LRU_BLOCK])
        is_.append(g[:, LRU_BLOCK:])
    rg = _sigmoid(jnp.concatenate(rs, axis=1) + ba_ref[...])
    ig = _sigmoid(jnp.concatenate(is_, axis=1) + bx_ref[...])
    log_a = -LRU_C * rg * _softplus(-lam_ref[...])
    a_scr[...] = jnp.exp(log_a)
    b_scr[...] = jnp.sqrt(1.0 - jnp.exp(2.0 * log_a)) * (ig * xc)

    rowi = lax.broadcasted_iota(jnp.int32, (SUBLANES, LRU_WIDTH), 0)

    def body(gi, carry):
        off = pl.multiple_of(gi * SUBLANES, SUBLANES)
        a = a_scr[pl.ds(off, SUBLANES), :]
        bv = b_scr[pl.ds(off, SUBLANES), :]
        for s in (1, 2, 4):
            a_sh = jnp.where(rowi >= s, pltpu.roll(a, s, 0), 1.0)
            b_sh = jnp.where(rowi >= s, pltpu.roll(bv, s, 0), 0.0)
            bv = a * b_sh + bv
            a = a * a_sh
        h = a * carry + bv
        b_scr[pl.ds(off, SUBLANES), :] = h
        return jnp.broadcast_to(h[SUBLANES - 1:SUBLANES, :], (SUBLANES, LRU_WIDTH))

    hcarry[...] = lax.fori_loop(0, r // SUBLANES, body, hcarry[...])
    yc_ref[...] = (b_scr[...] * pq_ref[:, 3 * SB_WIDTH:].astype(F32)).astype(yc_ref.dtype)

    def headnorm(x, g):
        sq = x * x
        hi = sq.astype(BF16)
        lo = (sq - hi.astype(F32)).astype(BF16)
        ms = _dot(hi, mavg_ref[...]) + _dot(lo, mavg_ref[...])
        return x * lax.rsqrt(ms + EPS) * g

    q = headnorm(pq_ref[:, 0:SB_WIDTH].astype(F32), qn_ref[...]) * (SB_HEAD_DIM ** -0.5 * LOG2E)
    q_ref[...] = q.astype(q_ref.dtype)
    kn = headnorm(pq_ref[:, SB_WIDTH:2 * SB_WIDTH].astype(F32), kn_ref[...])
    vt = pq_ref[:, 2 * SB_WIDTH:3 * SB_WIDTH].astype(F32).T
    pw = 2 * SB_HEAD_DIM
    lane = lax.broadcasted_iota(jnp.int32, (r, pw), 1)
    sub = lax.broadcasted_iota(jnp.int32, (pw, r), 0)
    for pr in range(SB_HEADS // 2):
        kp = kn[:, pr * pw:(pr + 1) * pw]
        k_ref[:, 2 * pr * pw:(2 * pr + 1) * pw] = jnp.where(lane < SB_HEAD_DIM, kp, 0.0).astype(k_ref.dtype)
        k_ref[:, (2 * pr + 1) * pw:(2 * pr + 2) * pw] = jnp.where(lane < SB_HEAD_DIM, 0.0, kp).astype(k_ref.dtype)
        vp = vt[pr * pw:(pr + 1) * pw, :]
        v_ref[2 * pr * pw:(2 * pr + 1) * pw, :] = jnp.where(sub < SB_HEAD_DIM, vp, 0.0).astype(v_ref.dtype)
        v_ref[(2 * pr + 1) * pw:(2 * pr + 2) * pw, :] = jnp.where(sub < SB_HEAD_DIM, 0.0, vp).astype(v_ref.dtype)


def _odd_core(pq, pc, wg, ba, bx, lam, qn, kn, mavg):
    b, t, _ = pq.shape
    r = min(512, t)
    full = lambda a: pl.BlockSpec(a.shape, lambda bi, ti: (0,) * a.ndim)
    smalls = (wg, ba, bx, lam, qn, kn, mavg)
    blk = lambda w: pl.BlockSpec((None, r, w), lambda bi, ti: (bi, ti, 0))
    return pl.pallas_call(
        _odd_core_kernel,
        grid=(b, t // r),
        in_specs=[blk(3 * SB_WIDTH + LRU_WIDTH), blk(LRU_WIDTH)] + [full(a) for a in smalls],
        out_specs=[blk(LRU_WIDTH), blk(SB_WIDTH), blk(2 * SB_WIDTH),
                   pl.BlockSpec((None, 2 * SB_WIDTH, r), lambda bi, ti: (bi, 0, ti))],
        out_shape=[jax.ShapeDtypeStruct((b, t, LRU_WIDTH), BF16), jax.ShapeDtypeStruct((b, t, SB_WIDTH), BF16),
                   jax.ShapeDtypeStruct((b, t, 2 * SB_WIDTH), BF16), jax.ShapeDtypeStruct((b, 2 * SB_WIDTH, t), BF16)],
        scratch_shapes=[
            pltpu.VMEM((r, LRU_WIDTH), F32),
            pltpu.VMEM((r, LRU_WIDTH), F32),
            pltpu.VMEM((SUBLANES, LRU_WIDTH), F32),
        ],
        compiler_params=_params(("parallel", "arbitrary")),
        name="odd_core",
    )(pq, pc, *smalls)


SB_TQ = 512
SB_TK = 128
SB_UNROLL = 2
SB_STAGES = 5


def _sb_attn_kernel(q_ref, k_ref, vt_ref, ut_ref, o_ref, z_buf, lb_buf, l1b_buf, r0_buf, r0_prev, c_buf, sfx_buf, w_buf, acc):
    qi = pl.program_id(2)
    tq = q_ref.shape[0]
    tk = SB_TK
    pw = 2 * SB_HEAD_DIM
    ndiag = tq // tk
    jmax = qi * ndiag + (ndiag - 1)
    kloc = lax.broadcasted_iota(jnp.int32, (tk, tq), 0)
    qpos = qi * tq + lax.broadcasted_iota(jnp.int32, (tk, tq), 1)

    def key_start(t):
        return pl.multiple_of((jmax - t) * tk, tk)

    def p0(t):
        kx = k_ref[pl.ds(key_start(t), tk), :]
        km = jnp.concatenate([kx[:, :pw], kx[:, pw:]], axis=0)
        z_buf[...] = _dot_nt(km, q_ref[...])

    def p1(t, slot, masked):
        if masked:
            valid = (key_start(t) + kloc) < qpos
        for h in range(2):
            z = z_buf[h * tk:(h + 1) * tk, :]
            neg_abs = lax.bitcast_convert_type(lax.bitcast_convert_type(z, jnp.int32) | SIGN_BIT, F32)
            sp = jnp.log(1.0 + jnp.exp2(neg_abs)) * LOG2E
            lb = jnp.minimum(z, 0.0) - sp
            l1 = lb - z
            if masked:
                l1 = jnp.where(valid, l1, 0.0)
            lb_buf[slot, h * tk:(h + 1) * tk, :] = lb
            l1b_buf[:, h * tq:(h + 1) * tq] = l1.astype(BF16)
            r0_buf[:, h * tq:(h + 1) * tq] = l1[0:1, :]

    def p2():
        sfx_buf[...] = _dot(ut_ref[...], l1b_buf[...])
        r0_prev[...] = r0_buf[...]

    def p3(t, slot, masked):
        if masked:
            valid = (key_start(t) + kloc) < qpos
        carry = c_buf[...]
        for h in range(2):
            sfx = sfx_buf[:, h * tq:(h + 1) * tq] + carry[:, h * tq:(h + 1) * tq]
            w = jnp.exp2(lb_buf[slot, h * tk:(h + 1) * tk, :] + sfx)
            if masked:
                w = jnp.where(valid, w, 0.0)
            w_buf[h * tk:(h + 1) * tk, :] = w.astype(BF16)
        c_buf[...] = carry + sfx_buf[0:1, :] + r0_prev[...]

    def p4(t):
        vx = vt_ref[:, pl.ds(key_start(t), tk)]
        vbd = jnp.concatenate([vx[:pw], vx[pw:]], axis=1)
        acc[...] += _dot(vbd, w_buf[...])

    def emit(i, active, parity, masked):
        if 4 in active:
            p4(i - 4)
        if 3 in active:
            p3(i - 3, (parity - 3) % 2, masked(i - 3))
        if 2 in active:
            p2()
        if 1 in active:
            p1(i - 1, (parity - 1) % 2, masked(i - 1))
        if 0 in active:
            p0(i)

    def static_iterations(n_tiles, iters):
        for i in iters:
            active = {k for k in range(SB_STAGES) if 0 <= i - k < n_tiles}
            emit(i, active, i % 2, lambda t: t < ndiag)

    acc[...] = jnp.zeros_like(acc)
    c_buf[...] = jnp.zeros_like(c_buf)

    @pl.when(qi == 0)
    def _diagonal_only():
        static_iterations(ndiag, range(ndiag + SB_STAGES - 1))

    @pl.when(qi > 0)
    def _full():
        n = (qi + 1) * ndiag
        head = 2 * ndiag
        static_iterations(head, range(head))
        never = lambda t: False

        def body(jj, _):
            for u in range(SB_UNROLL):
                emit(head + SB_UNROLL * jj + u, set(range(SB_STAGES)), u % 2, never)
            return 0

        lax.fori_loop(0, (n - head) // SB_UNROLL, body, 0)
        for e in range(SB_STAGES - 1):
            emit(n + e, {k for k in range(SB_STAGES) if k > e}, e % 2, never)

    o_ref[...] = acc[...].T.astype(o_ref.dtype)


def _sb_attention(q, kx, vtx, ut):
    b, t, _ = q.shape
    tq = min(SB_TQ, t)
    tk = SB_TK
    pairs = SB_HEADS // 2
    pw = 2 * SB_HEAD_DIM
    return pl.pallas_call(
        _sb_attn_kernel,
        grid=(b, pairs, t // tq),
        in_specs=[
            pl.BlockSpec((None, tq, pw), lambda bi, pi, qi: (bi, qi, pi)),
            pl.BlockSpec((None, t, 2 * pw), lambda bi, pi, qi: (bi, 0, pi)),
            pl.BlockSpec((None, 2 * pw, t), lambda bi, pi, qi: (bi, pi, 0)),
            pl.BlockSpec(ut.shape, lambda bi, pi, qi: (0, 0)),
        ],
        out_specs=pl.BlockSpec((None, tq, pw), lambda bi, pi, qi: (bi, qi, pi)),
        out_shape=jax.ShapeDtypeStruct((b, t, SB_WIDTH), BF16),
        scratch_shapes=[
            pltpu.VMEM((2 * tk, tq), F32),
            pltpu.VMEM((2, 2 * tk, tq), F32),
            pltpu.VMEM((tk, 2 * tq), BF16),
            pltpu.VMEM((1, 2 * tq), F32),
            pltpu.VMEM((1, 2 * tq), F32),
            pltpu.VMEM((1, 2 * tq), F32),
            pltpu.VMEM((tk, 2 * tq), F32),
            pltpu.VMEM((2 * tk, tq), BF16),
            pltpu.VMEM((pw, tq), F32),
        ],
        compiler_params=_params(("parallel", "parallel", "arbitrary")),
        name="sb_attention",
    )(q, kx, vtx, ut)


def _const_tables():
    i = np.arange(128)
    tril = (i[None, :] <= i[:, None]).astype(np.float32)
    e_mat = np.zeros((128, SSD_D_INNER), np.float32)
    for h in range(SSD_HEADS):
        e_mat[h, h * SSD_HEADDIM:(h + 1) * SSD_HEADDIM] = 1.0
    uo = (i[None, :] > i[:, None]).astype(np.float32)
    hd = np.arange(SB_WIDTH) // SB_HEAD_DIM
    mavg = (hd[:, None] == hd[None, :]).astype(np.float32) / SB_HEAD_DIM
    return (jnp.asarray(tril, BF16), jnp.asarray(e_mat, BF16), jnp.asarray(uo, BF16), jnp.asarray(mavg, BF16))


def _row(v):
    return v.reshape(1, -1).astype(F32)


def _pad_lanes(v, width=128):
    return jnp.pad(v.reshape(1, -1).astype(F32), ((0, 0), (0, width - v.shape[-1])))


def kernel(x, norm_mix, norm_mlp, mlp_w1, mlp_w2, ev_w_in, ev_w_out, ret_qn, ret_kn, ret_gn, ssd_conv_w, ssd_conv_b, ssd_dt_bias, ssd_a_log, ssd_d, ssd_norm, od_w_in, od_w_out, lru_conv_w, lru_conv_b, lru_wa, lru_ba, lru_wx, lru_bx, lru_lam, sb_qn, sb_kn):
    b, t, d = x.shape
    n = b * t
    depth = norm_mix.shape[0]
    tril, e_mat, uo, mavg = _const_tables()
    cos, sin = _rope_tables(t)
    xf = x.reshape(n, d)
    for l in range(depth):
        g_mix = _row(norm_mix[l])
        if l % 2 == 0:
            e = l // 2
            w_in = jnp.pad(ev_w_in[e], ((0, 0), (0, EVEN_IN_PAD - EVEN_IN))).astype(BF16)
            plan = ((EV_XBC, EV_DT, "conv_silu", 1, 0), (EV_G, EV_XBC, "silu", 0, EV_G),
                    (EV_DT, EVEN_IN_PAD, "none", 2, 0), (EV_Q, EV_G, "none", 0, 0))
            p, px, dt = _norm_proj(xf, g_mix, w_in, ssd_conv_w[e].astype(F32), _row(ssd_conv_b[e]), plan,
                                   (EV_XBC, SSD_CONV_DIM, EVEN_IN_PAD - EV_DT), (BF16, BF16, F32), t)
            xf = _even_core(
                p.reshape(b, t, EV_XBC), px.reshape(b, t, SSD_CONV_DIM), dt.reshape(b, t, EVEN_IN_PAD - EV_DT),
                xf.reshape(b, t, d), cos, sin,
                _row(ret_qn[e]), _row(ret_kn[e]), _row(ret_gn[e]), _pad_lanes(ssd_dt_bias[e]), _pad_lanes(ssd_a_log[e]),
                _row(jnp.repeat(ssd_d[e], SSD_HEADDIM)), _row(ssd_norm[e]), e_mat, tril,
                ev_w_out[e].astype(BF16)).reshape(n, d)
        else:
            o = l // 2
            w_in = od_w_in[o].astype(BF16)
            plan = ((OD_XC, OD_Q, "conv", 1, 0), (OD_GATE, OD_XC, "gelu", 0, ODD_IN - OD_Q), (OD_Q, ODD_IN, "none", 0, 0))
            pq, pc = _norm_proj(xf, g_mix, w_in, lru_conv_w[o].astype(F32), _row(lru_conv_b[o]), plan,
                                (ODD_IN - OD_Q + LRU_WIDTH, LRU_WIDTH), (BF16, BF16), t)
            wg = jnp.concatenate([lru_wa[o], lru_wx[o]], axis=-1).astype(BF16)
            yc, q, k, v = _odd_core(
                pq.reshape(b, t, 3 * SB_WIDTH + LRU_WIDTH), pc.reshape(b, t, LRU_WIDTH),
                wg, _row(lru_ba[o]), _row(lru_bx[o]),
                _row(lru_lam[o]), _row(jnp.tile(sb_qn[o], SB_HEADS)), _row(jnp.tile(sb_kn[o], SB_HEADS)), mavg)
            yd = _sb_attention(q, k, v, uo)
            w_out = od_w_out[o].astype(BF16)
            xf = _outproj(xf, [yc.reshape(n, LRU_WIDTH), yd.reshape(n, SB_WIDTH)], [w_out[:LRU_WIDTH], w_out[LRU_WIDTH:]])
        xf = _mlp(xf, _row(norm_mlp[l]), mlp_w1[l].astype(BF16), mlp_w2[l].astype(BF16))
    return xf.reshape(b, t, d)
```

```python
import functools
import math

import numpy as np
import jax
import jax.numpy as jnp
from jax import lax
from jax.experimental import pallas as pl
from jax.experimental.pallas import tpu as pltpu

F32 = jnp.float32
BF16 = jnp.bfloat16

D_MODEL = 1024
EPS = 1e-6

RET_HEADS = 4
RET_DK = 128
RET_DV = 256
ROPE_BASE = 10000.0
RET_LOG_GAMMA = tuple(float(np.log1p(-(2.0 ** (-5.0 - h)))) for h in range(RET_HEADS))

SSD_D_INNER = 1024
SSD_HEADDIM = 64
SSD_HEADS = 16
SSD_GROUPS = 2
SSD_STATE = 128
SSD_CONV = 4
SSD_CONV_DIM = SSD_D_INNER + 2 * SSD_GROUPS * SSD_STATE
CHUNK = 128
EVEN_ROWS = 512

LRU_WIDTH = 1024
LRU_BLOCKS = 8
LRU_BLOCK = 128
LRU_C = 8.0
LRU_CONV = 4

SB_HEADS = 8
SB_HEAD_DIM = 64
SB_WIDTH = SB_HEADS * SB_HEAD_DIM

D_FF = 4 * D_MODEL

EVEN_IN = 5648
EVEN_IN_PAD = 5760
EVEN_OUT = 2048
ODD_IN = 3584
ODD_OUT = 1536

EV_Q, EV_K, EV_V, EV_G, EV_Z, EV_XBC, EV_DT = 0, 512, 1024, 2048, 3072, 4096, 5632
OD_GATE, OD_XC, OD_Q, OD_K, OD_V = 0, 1024, 2048, 2560, 3072

LOG2E = 1.4426950408889634
SUBLANES = 8
CONV_PAD = 8
VMEM_LIMIT = 56 * 1024 * 1024


def _params(sem):
    return pltpu.CompilerParams(dimension_semantics=sem, vmem_limit_bytes=VMEM_LIMIT)


def _dot(a, b):
    return jnp.dot(a, b, preferred_element_type=F32)


def _dot_nt(a, b):
    return lax.dot_general(a, b, (((1,), (1,)), ((), ())), preferred_element_type=F32)


def _dot_tn(a, b):
    return lax.dot_general(a, b, (((0,), (0,)), ((), ())), preferred_element_type=F32)


def _split3(x):
    hi = x.astype(BF16)
    r = x - hi.astype(F32)
    mid = r.astype(BF16)
    lo = (r - mid.astype(F32)).astype(BF16)
    return hi, mid, lo


def _dot_exact_lhs(x, m):
    hi, mid, lo = _split3(x)
    return _dot(hi, m) + _dot(mid, m) + _dot(lo, m)


def _dot_exact_rhs(m, x):
    hi, mid, lo = _split3(x)
    return _dot(m, hi) + _dot(m, mid) + _dot(m, lo)


def _sigmoid(x):
    return 1.0 / (1.0 + jnp.exp(-x))


def _silu(x):
    return x * _sigmoid(x)


def _softplus(x):
    return jnp.maximum(x, 0.0) + jnp.log(1.0 + jnp.exp(-jnp.abs(x)))


def _gelu_tanh(x):
    c = math.sqrt(2.0 / math.pi)
    return x * (0.5 * (1.0 + jnp.tanh(c * (x + 0.044715 * (x * x * x)))))


def _rms(x, g):
    return x * lax.rsqrt(jnp.mean(x * x, axis=-1, keepdims=True) + EPS) * g


def _causal_conv(acc, cw_ref, cb_ref, xpad, tail, reset):
    rows = acc.shape[0]
    taps = cw_ref.shape[0]
    xpad[0:CONV_PAD, :] = jnp.where(reset, 0.0, tail[...])
    xpad[CONV_PAD:CONV_PAD + rows, :] = acc
    y = cb_ref[...]
    for k in range(taps):
        off = CONV_PAD - (taps - 1) + k
        y = y + cw_ref[k:k + 1, :] * xpad[off:off + rows, :]
    tail[...] = xpad[rows:rows + CONV_PAD, :]
    return y


_ACTS = {"none": lambda v: v, "silu": _silu, "gelu": _gelu_tanh}
PROJ_TM = 512


def _norm_proj_kernel(plan, n_out, tiles_per_seq, x_ref, g_ref, w_ref, cw_ref, cb_ref, *rest):
    outs, (xpad, tail) = rest[:n_out], rest[n_out:]
    h = _rms(x_ref[...], g_ref[...]).astype(BF16)
    for lo, hi, op, out, dst in plan:
        acc = _dot(h, w_ref[:, lo:hi])
        if op.startswith("conv"):
            acc = _causal_conv(acc, cw_ref, cb_ref, xpad, tail, pl.program_id(0) % tiles_per_seq == 0)
            op = "silu" if op == "conv_silu" else "none"
        outs[out][:, dst:dst + hi - lo] = _ACTS[op](acc).astype(outs[out].dtype)


def _norm_proj(x, g, w, layer, cw, cb, plan, out_widths, out_dtypes, seq_len):
    n, d = x.shape
    tm = min(PROJ_TM, n)
    row = lambda width: pl.BlockSpec((tm, width), lambda i: (i, 0))
    full = lambda a: pl.BlockSpec(a.shape, lambda i: (0, 0))
    resident = pl.BlockSpec((None,) + w.shape[1:], lambda i: (layer, 0, 0), pipeline_mode=pl.Buffered(1))
    cwidth = cw.shape[1]
    return pl.pallas_call(
        functools.partial(_norm_proj_kernel, plan, len(out_widths), seq_len // tm),
        grid=(n // tm,),
        in_specs=[row(d), full(g), resident, full(cw), full(cb)],
        out_specs=[row(wd) for wd in out_widths],
        out_shape=[jax.ShapeDtypeStruct((n, wd), dt) for wd, dt in zip(out_widths, out_dtypes)],
        scratch_shapes=[pltpu.VMEM((tm + CONV_PAD, cwidth), F32), pltpu.VMEM((CONV_PAD, cwidth), F32)],
        compiler_params=_params(("arbitrary",)),
        name="norm_proj",
    )(x, g, w, cw, cb)


def _mlp_kernel(x_ref, g_ref, w1_ref, w2_ref, o_ref, h_scr, acc_scr):
    f = pl.program_id(1)

    @pl.when(f == 0)
    def _():
        h_scr[...] = _rms(x_ref[...], g_ref[...]).astype(BF16)
        acc_scr[...] = jnp.zeros_like(acc_scr)

    a = _dot(h_scr[...], w1_ref[...])
    a = jnp.square(jnp.maximum(a, 0.0)).astype(BF16)
    acc_scr[...] += _dot(a, w2_ref[...])

    @pl.when(f == pl.num_programs(1) - 1)
    def _():
        o_ref[...] = x_ref[...] + acc_scr[...]


def _mlp(x, g, w1, w2, layer):
    n, d = x.shape
    dff = w1.shape[2]
    tm = min(1024, n)
    tf = 512
    return pl.pallas_call(
        _mlp_kernel,
        grid=(n // tm, dff // tf),
        in_specs=[
            pl.BlockSpec((tm, d), lambda i, f: (i, 0)),
            pl.BlockSpec((1, d), lambda i, f: (0, 0)),
            pl.BlockSpec((None, d, tf), lambda i, f: (layer, 0, f)),
            pl.BlockSpec((None, tf, d), lambda i, f: (layer, f, 0)),
        ],
        out_specs=pl.BlockSpec((tm, d), lambda i, f: (i, 0)),
        out_shape=jax.ShapeDtypeStruct((n, d), F32),
        scratch_shapes=[pltpu.VMEM((tm, d), BF16), pltpu.VMEM((tm, d), F32)],
        compiler_params=_params(("parallel", "arbitrary")),
        name="mlp",
    )(x, g, w1, w2)


def _outproj2_kernel(x_ref, ya_ref, yb_ref, wa_ref, wb_ref, o_ref):
    o_ref[...] = x_ref[...] + _dot(ya_ref[...], wa_ref[...]) + _dot(yb_ref[...], wb_ref[...])


def _outproj2(x, ya, yb, w, layer):
    n, d = x.shape
    ka, kb = ya.shape[1], yb.shape[1]
    assert ka % kb == 0
    tm = min(512, n)
    row = lambda width: pl.BlockSpec((tm, width), lambda i: (i, 0))
    return pl.pallas_call(
        _outproj2_kernel,
        grid=(n // tm,),
        in_specs=[row(d), row(ka), row(kb),
                  pl.BlockSpec((None, ka, d), lambda i: (layer, 0, 0)),
                  pl.BlockSpec((None, kb, d), lambda i: (layer, ka // kb, 0))],
        out_specs=row(d),
        out_shape=jax.ShapeDtypeStruct((n, d), F32),
        compiler_params=_params(("parallel",)),
        name="outproj",
    )(x, ya, yb, w, w)


def _rope_kernel(inv_ref, cos_ref, sin_ref):
    rows = cos_ref.shape[0]
    pos = (pl.program_id(0) * rows + lax.broadcasted_iota(jnp.int32, (rows, RET_DK), 0)).astype(F32)
    lane = lax.broadcasted_iota(jnp.int32, (rows, RET_DK), 1)
    ang = pos * inv_ref[...]
    cos_ref[...] = jnp.cos(ang)
    sin_ref[...] = jnp.where(lane < RET_DK // 2, -jnp.sin(ang), jnp.sin(ang))


def _rope_tables(t):
    half = RET_DK // 2
    inv = ROPE_BASE ** (-jnp.arange(half, dtype=F32) / half)
    inv = jnp.concatenate([inv, inv])[None, :]
    rows = min(512, t)
    return pl.pallas_call(
        _rope_kernel,
        grid=(t // rows,),
        in_specs=[pl.BlockSpec((1, RET_DK), lambda i: (0, 0))],
        out_specs=[pl.BlockSpec((rows, RET_DK), lambda i: (i, 0))] * 2,
        out_shape=[jax.ShapeDtypeStruct((t, RET_DK), F32)] * 2,
        compiler_params=_params(("parallel",)),
        name="rope_tables",
    )(inv)


def _even_core_kernel(p_ref, px_ref, dt_ref, x_ref, cos_ref, sin_ref, qn_ref, kn_ref, gn_ref, dtb_ref, alog_ref,
                      dsk_ref, sg_ref, e_ref, tril_ref, wout_ref, o_ref, rstate, sstate, decay_scr, y_scr):
    t = pl.program_id(1)
    c = CHUNK
    row = lax.broadcasted_iota(jnp.int32, (c, c), 0)
    col = lax.broadcasted_iota(jnp.int32, (c, c), 1)
    causal = row >= col

    @pl.when(t == 0)
    def _init():
        rstate[...] = jnp.zeros_like(rstate)
        sstate[...] = jnp.zeros_like(sstate)
        y_scr[...] = jnp.zeros_like(y_scr)
        rel = (row - col).astype(F32)
        for h in range(RET_HEADS):
            decay_scr[h] = jnp.where(causal, jnp.exp(RET_LOG_GAMMA[h] * jnp.maximum(rel, 0.0)), 0.0)

    slot = t % 2
    o_ref[...] = x_ref[...] + _dot(y_scr[1 - slot], wout_ref[...])
    y_ref = y_scr.at[slot]
    idx = lax.broadcasted_iota(jnp.int32, (c, 1), 0).astype(F32)
    for ci in range(p_ref.shape[0] // c):
        _even_chunk(slice(ci * c, (ci + 1) * c), idx, causal, p_ref, px_ref, dt_ref, cos_ref, sin_ref, qn_ref, kn_ref,
                    gn_ref, dtb_ref, alog_ref, dsk_ref, sg_ref, e_ref, tril_ref, y_ref, rstate, sstate, decay_scr)


def _even_chunk(rows, idx, causal, p_ref, px_ref, dt_ref, cos_ref, sin_ref, qn_ref, kn_ref, gn_ref, dtb_ref, alog_ref,
                dsk_ref, sg_ref, e_ref, tril_ref, y_ref, rstate, sstate, decay_scr):
    c = CHUNK
    cos = cos_ref[rows, :]
    sin = sin_ref[rows, :]
    for h in range(RET_HEADS):
        lg = RET_LOG_GAMMA[h]
        qh = _rms(p_ref[rows, EV_Q + h * RET_DK:EV_Q + (h + 1) * RET_DK].astype(F32), qn_ref[...])
        kh = _rms(p_ref[rows, EV_K + h * RET_DK:EV_K + (h + 1) * RET_DK].astype(F32), kn_ref[...])
        qh = qh * cos + pltpu.roll(qh, RET_DK // 2, 1) * sin
        kh = (kh * cos + pltpu.roll(kh, RET_DK // 2, 1) * sin) * (RET_DK ** -0.5)
        vb = p_ref[rows, EV_V + h * RET_DV:EV_V + (h + 1) * RET_DV]
        scores = _dot_nt(qh.astype(BF16), kh.astype(BF16)) * decay_scr[h]
        inner = _dot(scores.astype(BF16), vb)
        s_prev = rstate[h]
        q_dec = jnp.exp(lg * (idx + 1.0))
        cross = _dot((qh * q_dec).astype(BF16), s_prev.astype(BF16))
        k_dec = jnp.exp(lg * (float(c - 1) - idx))
        rstate[h] = s_prev * math.exp(lg * c) + _dot_tn((kh * k_dec).astype(BF16), vb)
        ya = inner + cross
        yc = ya - jnp.mean(ya, axis=-1, keepdims=True)
        yn = yc * lax.rsqrt(jnp.mean(yc * yc, axis=-1, keepdims=True) + EPS)
        yn = yn * gn_ref[:, h * RET_DV:(h + 1) * RET_DV]
        gate = p_ref[rows, EV_G + h * RET_DV:EV_G + (h + 1) * RET_DV].astype(F32)
        y_ref[rows, h * RET_DV:(h + 1) * RET_DV] = (gate * yn).astype(y_ref.dtype)

    xs = px_ref[rows, :SSD_D_INNER].astype(F32)
    gs = SSD_GROUPS * SSD_STATE
    bm = px_ref[rows, SSD_D_INNER:SSD_D_INNER + gs]
    cm = px_ref[rows, SSD_D_INNER + gs:SSD_D_INNER + 2 * gs]

    dt = _softplus(dt_ref[rows, :] + dtb_ref[...])
    a_neg = -jnp.exp(alog_ref[...])
    d_a = dt * a_neg
    acum = _dot_exact_rhs(tril_ref[...], d_a)
    acum_t = acum.T
    exp_a = jnp.exp(acum)
    dec = jnp.exp(acum[c - 1:c, :] - acum)
    e_mat = e_ref[...]
    dt_e = _dot_exact_lhs(dt, e_mat)
    exp_a_e = _dot_exact_lhs(exp_a, e_mat)
    dec_e = _dot_exact_lhs(dec, e_mat)
    xr = xs * dt_e
    xrd_b = (xr * dec_e).astype(BF16)
    cdec_e = exp_a_e[c - 1:c, :]
    lane = lax.broadcasted_iota(jnp.int32, (c, 128), 1)
    left = lane < SSD_HEADDIM
    hpg = SSD_HEADS // SSD_GROUPS
    gw = hpg * SSD_HEADDIM
    z_gate = p_ref[rows, EV_Z:EV_Z + SSD_D_INNER].astype(F32)
    for g in range(SSD_GROUPS):
        bg = bm[:, g * SSD_STATE:(g + 1) * SSD_STATE]
        cg = cm[:, g * SSD_STATE:(g + 1) * SSD_STATE]
        cb_mat = _dot_nt(cg, bg)
        s_prev = sstate[g]
        y_off = _dot(cg, s_prev.astype(BF16)) * exp_a_e[:, g * gw:(g + 1) * gw]
        pairs = []
        for pr in range(hpg // 2):
            h0 = g * hpg + 2 * pr
            ms = []
            for h in (h0, h0 + 1):
                seg = acum[:, h:h + 1] - acum_t[h:h + 1, :]
                l_mat = jnp.where(causal, jnp.exp(jnp.minimum(seg, 0.0)), 0.0)
                ms.append((cb_mat * l_mat).astype(BF16))
            xp = xr[:, h0 * SSD_HEADDIM:(h0 + 2) * SSD_HEADDIM]
            xbd = jnp.concatenate([jnp.where(left, xp, 0.0), jnp.where(left, 0.0, xp)], axis=0).astype(BF16)
            pairs.append(_dot(jnp.concatenate(ms, axis=1), xbd))
        y_diag = jnp.concatenate(pairs, axis=1)
        sstate[g] = s_prev * cdec_e[:, g * gw:(g + 1) * gw] + _dot_tn(bg, xrd_b[:, g * gw:(g + 1) * gw])
        yb = y_diag + y_off + xs[:, g * gw:(g + 1) * gw] * dsk_ref[:, g * gw:(g + 1) * gw]
        yb = yb * z_gate[:, g * gw:(g + 1) * gw]
        yb = _rms(yb, sg_ref[:, g * gw:(g + 1) * gw])
        lo = RET_HEADS * RET_DV + g * gw
        y_ref[rows, lo:lo + gw] = yb.astype(y_ref.dtype)


def _even_core(p, px, dt, x, cos, sin, qn, kn, gn, dtb, alog, dsk, sg, e_mat, tril, w_out):
    b, t, _ = p.shape
    c = min(EVEN_ROWS, t)
    last = t // c - 1
    cur = lambda bi, ti: (bi, jnp.minimum(ti, last), 0)
    prev = lambda bi, ti: (bi, jnp.maximum(ti - 1, 0), 0)
    full = lambda a: pl.BlockSpec(a.shape, lambda bi, ti: (0,) * a.ndim)
    smalls = (qn, kn, gn, dtb, alog, dsk, sg, e_mat, tril, w_out)
    return pl.pallas_call(
        _even_core_kernel,
        grid=(b, t // c + 1),
        in_specs=[
            pl.BlockSpec((None, c, EV_XBC), cur),
            pl.BlockSpec((None, c, SSD_CONV_DIM), cur),
            pl.BlockSpec((None, c, EVEN_IN_PAD - EV_DT), cur),
            pl.BlockSpec((None, c, D_MODEL), prev),
            pl.BlockSpec((c, RET_DK), lambda bi, ti: (jnp.minimum(ti, last), 0)),
            pl.BlockSpec((c, RET_DK), lambda bi, ti: (jnp.minimum(ti, last), 0)),
        ] + [full(a) for a in smalls],
        out_specs=pl.BlockSpec((None, c, D_MODEL), prev),
        out_shape=jax.ShapeDtypeStruct((b, t, D_MODEL), F32),
        scratch_shapes=[
            pltpu.VMEM((RET_HEADS, RET_DK, RET_DV), F32),
            pltpu.VMEM((SSD_GROUPS, SSD_STATE, SSD_D_INNER // SSD_GROUPS), F32),
            pltpu.VMEM((RET_HEADS, CHUNK, CHUNK), F32),
            pltpu.VMEM((2, c, EVEN_OUT), BF16),
        ],
        compiler_params=_params(("parallel", "arbitrary")),
        name="even_core",
    )(p, px, dt, x, cos, sin, *smalls)


def _odd_core_kernel(pq_ref, pc_ref, wg_ref, ba_ref, bx_ref, lam_ref, qn_ref, kn_ref, mavg_ref,
                     yc_ref, q_ref, k_ref, v_ref, a_scr, b_scr, hcarry):
    t = pl.program_id(1)
    r = pc_ref.shape[0]

    @pl.when(t == 0)
    def _init():
        hcarry[...] = jnp.zeros_like(hcarry)

    xc = pc_ref[...].astype(F32)

    rs, is_ = [], []
    for blk in range(LRU_BLOCKS):
        g = _dot(xc[:, blk * LRU_BLOCK:(blk + 1) * LRU_BLOCK].astype(BF16), wg_ref[blk])
        rs.append(g[:, :LRU_BLOCK])
        is_.append(g[:, LRU_BLOCK:])
    rg = _sigmoid(jnp.concatenate(rs, axis=1) + ba_ref[...])
    ig = _sigmoid(jnp.concatenate(is_, axis=1) + bx_ref[...])
    log_a = -LRU_C * rg * _softplus(-lam_ref[...])
    a_scr[...] = jnp.exp(log_a)
    b_scr[...] = jnp.sqrt(1.0 - jnp.exp(2.0 * log_a)) * (ig * xc)

    rowi = lax.broadcasted_iota(jnp.int32, (SUBLANES, LRU_WIDTH), 0)

    def body(gi, carry):
        off = pl.multiple_of(gi * SUBLANES, SUBLANES)
        a = a_scr[pl.ds(off, SUBLANES), :]
        bv = b_scr[pl.ds(off, SUBLANES), :]
        for s in (1, 2, 4):
            a_sh = jnp.where(rowi >= s, pltpu.roll(a, s, 0), 1.0)
            b_sh = jnp.where(rowi >= s, pltpu.roll(bv, s, 0), 0.0)
            bv = a * b_sh + bv
            a = a * a_sh
        h = a * carry + bv
        b_scr[pl.ds(off, SUBLANES), :] = h
        return jnp.broadcast_to(h[SUBLANES - 1:SUBLANES, :], (SUBLANES, LRU_WIDTH))

    hcarry[...] = lax.fori_loop(0, r // SUBLANES, body, hcarry[...])
    yc_ref[...] = (b_scr[...] * pq_ref[:, 3 * SB_WIDTH:].astype(F32)).astype(yc_ref.dtype)

    def headnorm(x, g):
        sq = x * x
        hi = sq.astype(BF16)
        lo = (sq - hi.astype(F32)).astype(BF16)
        ms = _dot(hi, mavg_ref[...]) + _dot(lo, mavg_ref[...])
        return x * lax.rsqrt(ms + EPS) * g

    q = headnorm(pq_ref[:, 0:SB_WIDTH].astype(F32), qn_ref[...]) * (SB_HEAD_DIM ** -0.5 * LOG2E)
    q_ref[...] = q.astype(q_ref.dtype)
    kn = headnorm(pq_ref[:, SB_WIDTH:2 * SB_WIDTH].astype(F32), kn_ref[...])
    vt = pq_ref[:, 2 * SB_WIDTH:3 * SB_WIDTH].astype(F32).T
    pw = 2 * SB_HEAD_DIM
    lane = lax.broadcasted_iota(jnp.int32, (r, pw), 1)
    sub = lax.broadcasted_iota(jnp.int32, (pw, r), 0)
    for pr in range(SB_HEADS // 2):
        kp = kn[:, pr * pw:(pr + 1) * pw]
        k_ref[:, 2 * pr * pw:(2 * pr + 1) * pw] = jnp.where(lane < SB_HEAD_DIM, kp, 0.0).astype(k_ref.dtype)
        k_ref[:, (2 * pr + 1) * pw:(2 * pr + 2) * pw] = jnp.where(lane < SB_HEAD_DIM, 0.0, kp).astype(k_ref.dtype)
        vp = vt[pr * pw:(pr + 1) * pw, :]
        v_ref[2 * pr * pw:(2 * pr + 1) * pw, :] = jnp.where(sub < SB_HEAD_DIM, vp, 0.0).astype(v_ref.dtype)
        v_ref[(2 * pr + 1) * pw:(2 * pr + 2) * pw, :] = jnp.where(sub < SB_HEAD_DIM, 0.0, vp).astype(v_ref.dtype)


def _odd_core(pq, pc, wg, ba, bx, lam, qn, kn, mavg):
    b, t, _ = pq.shape
    r = min(512, t)
    full = lambda a: pl.BlockSpec(a.shape, lambda bi, ti: (0,) * a.ndim)
    smalls = (wg, ba, bx, lam, qn, kn, mavg)
    blk = lambda w: pl.BlockSpec((None, r, w), lambda bi, ti: (bi, ti, 0))
    return pl.pallas_call(
        _odd_core_kernel,
        grid=(b, t // r),
        in_specs=[blk(3 * SB_WIDTH + LRU_WIDTH), blk(LRU_WIDTH)] + [full(a) for a in smalls],
        out_specs=[blk(LRU_WIDTH), blk(SB_WIDTH), blk(2 * SB_WIDTH),
                   pl.BlockSpec((None, 2 * SB_WIDTH, r), lambda bi, ti: (bi, 0, ti))],
        out_shape=[jax.ShapeDtypeStruct((b, t, LRU_WIDTH), BF16), jax.ShapeDtypeStruct((b, t, SB_WIDTH), BF16),
                   jax.ShapeDtypeStruct((b, t, 2 * SB_WIDTH), BF16), jax.ShapeDtypeStruct((b, 2 * SB_WIDTH, t), BF16)],
        scratch_shapes=[
            pltpu.VMEM((r, LRU_WIDTH), F32),
            pltpu.VMEM((r, LRU_WIDTH), F32),
            pltpu.VMEM((SUBLANES, LRU_WIDTH), F32),
        ],
        compiler_params=_params(("parallel", "arbitrary")),
        name="odd_core",
    )(pq, pc, *smalls)


SB_TQ = 512
SB_TK = 128
SB_UNROLL = 2
SB_STAGES = 5


def _sb_attn_kernel(q_ref, k_ref, vt_ref, ut_ref, o_ref, z_buf, lb_buf, l1b_buf, r0_buf, r0_prev, c_buf, sfx_buf, w_buf, acc):
    qi = pl.program_id(2)
    tq = q_ref.shape[0]
    tk = SB_TK
    pw = 2 * SB_HEAD_DIM
    ndiag = tq // tk
    jmax = qi * ndiag + (ndiag - 1)
    kloc = lax.broadcasted_iota(jnp.int32, (tk, tq), 0)
    qpos = qi * tq + lax.broadcasted_iota(jnp.int32, (tk, tq), 1)

    def key_start(t):
        return pl.multiple_of((jmax - t) * tk, tk)

    def p0(t):
        kx = k_ref[pl.ds(key_start(t), tk), :]
        km = jnp.concatenate([kx[:, :pw], kx[:, pw:]], axis=0)
        z_buf[...] = _dot_nt(km, q_ref[...])

    def p1(t, slot, masked):
        if masked:
            valid = (key_start(t) + kloc) < qpos
        for h in range(2):
            z = z_buf[h * tk:(h + 1) * tk, :]
            sp = jnp.log(1.0 + jnp.exp2(-jnp.abs(z))) * LOG2E
            lb = jnp.minimum(z, 0.0) - sp
            l1 = lb - z
            if masked:
                l1 = jnp.where(valid, l1, 0.0)
            lb_buf[slot, h * tk:(h + 1) * tk, :] = lb
            l1b_buf[:, h * tq:(h + 1) * tq] = l1.astype(BF16)
            r0_buf[:, h * tq:(h + 1) * tq] = l1[0:1, :]

    def p2():
        sfx_buf[...] = _dot(ut_ref[...], l1b_buf[...])
        r0_prev[...] = r0_buf[...]

    def p3(t, slot, masked):
        if masked:
            valid = (key_start(t) + kloc) < qpos
        carry = c_buf[...]
        for h in range(2):
            sfx = sfx_buf[:, h * tq:(h + 1) * tq] + carry[:, h * tq:(h + 1) * tq]
            w = jnp.exp2(lb_buf[slot, h * tk:(h + 1) * tk, :] + sfx)
            if masked:
                w = jnp.where(valid, w, 0.0)
            w_buf[h * tk:(h + 1) * tk, :] = w.astype(BF16)
        c_buf[...] = carry + sfx_buf[0:1, :] + r0_prev[...]

    def p4(t):
        vx = vt_ref[:, pl.ds(key_start(t), tk)]
        vbd = jnp.concatenate([vx[:pw], vx[pw:]], axis=1)
        acc[...] += _dot(vbd, w_buf[...])

    def emit(i, active, parity, masked):
        if 4 in active:
            p4(i - 4)
        if 3 in active:
            p3(i - 3, (parity - 3) % 2, masked(i - 3))
        if 2 in active:
            p2()
        if 1 in active:
            p1(i - 1, (parity - 1) % 2, masked(i - 1))
        if 0 in active:
            p0(i)

    def static_iterations(n_tiles, iters):
        for i in iters:
            active = {k for k in range(SB_STAGES) if 0 <= i - k < n_tiles}
            emit(i, active, i % 2, lambda t: t < ndiag)

    acc[...] = jnp.zeros_like(acc)
    c_buf[...] = jnp.zeros_like(c_buf)

    @pl.when(qi == 0)
    def _diagonal_only():
        static_iterations(ndiag, range(ndiag + SB_STAGES - 1))

    @pl.when(qi > 0)
    def _full():
        n = (qi + 1) * ndiag
        head = 2 * ndiag
        static_iterations(head, range(head))
        never = lambda t: False

        def body(jj, _):
            for u in range(SB_UNROLL):
                emit(head + SB_UNROLL * jj + u, set(range(SB_STAGES)), u % 2, never)
            return 0

        lax.fori_loop(0, (n - head) // SB_UNROLL, body, 0)
        for e in range(SB_STAGES - 1):
            emit(n + e, {k for k in range(SB_STAGES) if k > e}, e % 2, never)

    o_ref[...] = acc[...].T.astype(o_ref.dtype)


def _sb_attention(q, kx, vtx, ut):
    b, t, _ = q.shape
    tq = min(SB_TQ, t)
    tk = SB_TK
    pairs = SB_HEADS // 2
    pw = 2 * SB_HEAD_DIM
    return pl.pallas_call(
        _sb_attn_kernel,
        grid=(b, pairs, t // tq),
        in_specs=[
            pl.BlockSpec((None, tq, pw), lambda bi, pi, qi: (bi, qi, pi)),
            pl.BlockSpec((None, t, 2 * pw), lambda bi, pi, qi: (bi, 0, pi)),
            pl.BlockSpec((None, 2 * pw, t), lambda bi, pi, qi: (bi, pi, 0)),
            pl.BlockSpec(ut.shape, lambda bi, pi, qi: (0, 0)),
        ],
        out_specs=pl.BlockSpec((None, tq, pw), lambda bi, pi, qi: (bi, qi, pi)),
        out_shape=jax.ShapeDtypeStruct((b, t, SB_WIDTH), BF16),
        scratch_shapes=[
            pltpu.VMEM((2 * tk, tq), F32),
            pltpu.VMEM((2, 2 * tk, tq), F32),
            pltpu.VMEM((tk, 2 * tq), BF16),
            pltpu.VMEM((1, 2 * tq), F32),
            pltpu.VMEM((1, 2 * tq), F32),
            pltpu.VMEM((1, 2 * tq), F32),
            pltpu.VMEM((tk, 2 * tq), F32),
            pltpu.VMEM((2 * tk, tq), BF16),
            pltpu.VMEM((pw, tq), F32),
        ],
        compiler_params=_params(("parallel", "parallel", "arbitrary")),
        name="sb_attention",
    )(q, kx, vtx, ut)


def _const_tables():
    i = np.arange(128)
    tril = (i[None, :] <= i[:, None]).astype(np.float32)
    e_mat = np.zeros((128, SSD_D_INNER), np.float32)
    for h in range(SSD_HEADS):
        e_mat[h, h * SSD_HEADDIM:(h + 1) * SSD_HEADDIM] = 1.0
    uo = (i[None, :] > i[:, None]).astype(np.float32)
    hd = np.arange(SB_WIDTH) // SB_HEAD_DIM
    mavg = (hd[:, None] == hd[None, :]).astype(np.float32) / SB_HEAD_DIM
    return (jnp.asarray(tril, BF16), jnp.asarray(e_mat, BF16), jnp.asarray(uo, BF16), jnp.asarray(mavg, BF16))


def _row(v):
    return v.reshape(1, -1).astype(F32)


def _pad_lanes(v, width=128):
    return jnp.pad(v.reshape(1, -1).astype(F32), ((0, 0), (0, width - v.shape[-1])))


def kernel(x, norm_mix, norm_mlp, mlp_w1, mlp_w2, ev_w_in, ev_w_out, ret_qn, ret_kn, ret_gn, ssd_conv_w, ssd_conv_b, ssd_dt_bias, ssd_a_log, ssd_d, ssd_norm, od_w_in, od_w_out, lru_conv_w, lru_conv_b, lru_wa, lru_ba, lru_wx, lru_bx, lru_lam, sb_qn, sb_kn):
    b, t, d = x.shape
    n = b * t
    depth = norm_mix.shape[0]
    tril, e_mat, uo, mavg = _const_tables()
    cos, sin = _rope_tables(t)
    xf = x.reshape(n, d)
    mlp_w1b, mlp_w2b = mlp_w1.astype(BF16), mlp_w2.astype(BF16)
    ev_w_in_b = jnp.pad(ev_w_in, ((0, 0), (0, 0), (0, EVEN_IN_PAD - EVEN_IN))).astype(BF16)
    od_w_in_b, od_w_out_b = od_w_in.astype(BF16), od_w_out.astype(BF16)
    for l in range(depth):
        g_mix = _row(norm_mix[l])
        if l % 2 == 0:
            e = l // 2
            plan = ((EV_XBC, EV_DT, "conv_silu", 1, 0), (EV_G, EV_XBC, "silu", 0, EV_G),
                    (EV_DT, EVEN_IN_PAD, "none", 2, 0), (EV_Q, EV_G, "none", 0, 0))
            p, px, dt = _norm_proj(xf, g_mix, ev_w_in_b, e, ssd_conv_w[e].astype(F32), _row(ssd_conv_b[e]), plan,
                                   (EV_XBC, SSD_CONV_DIM, EVEN_IN_PAD - EV_DT), (BF16, BF16, F32), t)
            xf = _even_core(
                p.reshape(b, t, EV_XBC), px.reshape(b, t, SSD_CONV_DIM), dt.reshape(b, t, EVEN_IN_PAD - EV_DT),
                xf.reshape(b, t, d), cos, sin,
                _row(ret_qn[e]), _row(ret_kn[e]), _row(ret_gn[e]), _pad_lanes(ssd_dt_bias[e]), _pad_lanes(ssd_a_log[e]),
                _row(jnp.repeat(ssd_d[e], SSD_HEADDIM)), _row(ssd_norm[e]), e_mat, tril,
                ev_w_out[e].astype(BF16)).reshape(n, d)
        else:
            o = l // 2
            plan = ((OD_XC, OD_Q, "conv", 1, 0), (OD_GATE, OD_XC, "gelu", 0, ODD_IN - OD_Q), (OD_Q, ODD_IN, "none", 0, 0))
            pq, pc = _norm_proj(xf, g_mix, od_w_in_b, o, lru_conv_w[o].astype(F32), _row(lru_conv_b[o]), plan,
                                (ODD_IN - OD_Q + LRU_WIDTH, LRU_WIDTH), (BF16, BF16), t)
            wg = jnp.concatenate([lru_wa[o], lru_wx[o]], axis=-1).astype(BF16)
            yc, q, k, v = _odd_core(
                pq.reshape(b, t, 3 * SB_WIDTH + LRU_WIDTH), pc.reshape(b, t, LRU_WIDTH),
                wg, _row(lru_ba[o]), _row(lru_bx[o]),
                _row(lru_lam[o]), _row(jnp.tile(sb_qn[o], SB_HEADS)), _row(jnp.tile(sb_kn[o], SB_HEADS)), mavg)
            yd = _sb_attention(q, k, v, uo)
            xf = _outproj2(xf, yc.reshape(n, LRU_WIDTH), yd.reshape(n, SB_WIDTH), od_w_out_b, o)
        xf = _mlp(xf, _row(norm_mlp[l]), mlp_w1b, mlp_w2b, l)
    return xf.reshape(b, t, d)
```

```python
import functools
import math

import numpy as np
import jax
import jax.numpy as jnp
from jax import lax
from jax.experimental import pallas as pl
from jax.experimental.pallas import tpu as pltpu

F32 = jnp.float32
BF16 = jnp.bfloat16

D_MODEL = 1024
EPS = 1e-6

RET_HEADS = 4
RET_DK = 128
RET_DV = 256
ROPE_BASE = 10000.0
RET_LOG_GAMMA = tuple(float(np.log1p(-(2.0 ** (-5.0 - h)))) for h in range(RET_HEADS))

SSD_D_INNER = 1024
SSD_HEADDIM = 64
SSD_HEADS = 16
SSD_GROUPS = 2
SSD_STATE = 128
SSD_CONV = 4
SSD_CONV_DIM = SSD_D_INNER + 2 * SSD_GROUPS * SSD_STATE
CHUNK = 128
EVEN_ROWS = 512

LRU_WIDTH = 1024
LRU_BLOCKS = 8
LRU_BLOCK = 128
LRU_C = 8.0
LRU_CONV = 4

SB_HEADS = 8
SB_HEAD_DIM = 64
SB_WIDTH = SB_HEADS * SB_HEAD_DIM

D_FF = 4 * D_MODEL

EVEN_IN = 5648
EVEN_IN_PAD = 5760
EVEN_OUT = 2048
ODD_IN = 3584
ODD_OUT = 1536

EV_Q, EV_K, EV_V, EV_G, EV_Z, EV_XBC, EV_DT = 0, 512, 1024, 2048, 3072, 4096, 5632
OD_GATE, OD_XC, OD_Q, OD_K, OD_V = 0, 1024, 2048, 2560, 3072

LOG2E = 1.4426950408889634
SUBLANES = 8
CONV_PAD = 8
VMEM_LIMIT = 56 * 1024 * 1024


def _params(sem):
    return pltpu.CompilerParams(dimension_semantics=sem, vmem_limit_bytes=VMEM_LIMIT)


def _dot(a, b):
    return jnp.dot(a, b, preferred_element_type=F32)


def _dot_nt(a, b):
    return lax.dot_general(a, b, (((1,), (1,)), ((), ())), preferred_element_type=F32)


def _dot_tn(a, b):
    return lax.dot_general(a, b, (((0,), (0,)), ((), ())), preferred_element_type=F32)


def _split3(x):
    hi = x.astype(BF16)
    r = x - hi.astype(F32)
    mid = r.astype(BF16)
    lo = (r - mid.astype(F32)).astype(BF16)
    return hi, mid, lo


def _dot_exact_lhs(x, m):
    hi, mid, lo = _split3(x)
    return _dot(hi, m) + _dot(mid, m) + _dot(lo, m)


def _dot_exact_rhs(m, x):
    hi, mid, lo = _split3(x)
    return _dot(m, hi) + _dot(m, mid) + _dot(m, lo)


def _sigmoid(x):
    return 1.0 / (1.0 + jnp.exp(-x))


def _silu(x):
    return x * _sigmoid(x)


def _softplus(x):
    return jnp.maximum(x, 0.0) + jnp.log(1.0 + jnp.exp(-jnp.abs(x)))


def _gelu_tanh(x):
    c = math.sqrt(2.0 / math.pi)
    return x * (0.5 * (1.0 + jnp.tanh(c * (x + 0.044715 * (x * x * x)))))


def _rms(x, g):
    return x * lax.rsqrt(jnp.mean(x * x, axis=-1, keepdims=True) + EPS) * g


def _causal_conv(acc, cw_ref, cb_ref, xpad, tail, reset):
    rows = acc.shape[0]
    taps = cw_ref.shape[0]
    xpad[0:CONV_PAD, :] = jnp.where(reset, 0.0, tail[...])
    xpad[CONV_PAD:CONV_PAD + rows, :] = acc
    y = cb_ref[...]
    for k in range(taps):
        off = CONV_PAD - (taps - 1) + k
        y = y + cw_ref[k:k + 1, :] * xpad[off:off + rows, :]
    tail[...] = xpad[rows:rows + CONV_PAD, :]
    return y


_ACTS = {"none": lambda v: v, "silu": _silu, "gelu": _gelu_tanh}
PROJ_TM = 512


def _norm_proj_kernel(plan, n_out, tiles_per_seq, x_ref, g_ref, w_ref, cw_ref, cb_ref, *rest):
    outs, (xpad, tail) = rest[:n_out], rest[n_out:]
    h = _rms(x_ref[...], g_ref[...]).astype(BF16)
    for lo, hi, op, out, dst in plan:
        acc = _dot(h, w_ref[:, lo:hi])
        if op.startswith("conv"):
            acc = _causal_conv(acc, cw_ref, cb_ref, xpad, tail, pl.program_id(0) % tiles_per_seq == 0)
            op = "silu" if op == "conv_silu" else "none"
        outs[out][:, dst:dst + hi - lo] = _ACTS[op](acc).astype(outs[out].dtype)


def _norm_proj(x, g, w, layer, cw, cb, plan, out_widths, out_dtypes, seq_len):
    n, d = x.shape
    tm = min(PROJ_TM, n)
    row = lambda width: pl.BlockSpec((tm, width), lambda i: (i, 0))
    full = lambda a: pl.BlockSpec(a.shape, lambda i: (0, 0))
    resident = pl.BlockSpec((None,) + w.shape[1:], lambda i: (layer, 0, 0), pipeline_mode=pl.Buffered(1))
    cwidth = cw.shape[1]
    return pl.pallas_call(
        functools.partial(_norm_proj_kernel, plan, len(out_widths), seq_len // tm),
        grid=(n // tm,),
        in_specs=[row(d), full(g), resident, full(cw), full(cb)],
        out_specs=[row(wd) for wd in out_widths],
        out_shape=[jax.ShapeDtypeStruct((n, wd), dt) for wd, dt in zip(out_widths, out_dtypes)],
        scratch_shapes=[pltpu.VMEM((tm + CONV_PAD, cwidth), F32), pltpu.VMEM((CONV_PAD, cwidth), F32)],
        compiler_params=_params(("arbitrary",)),
        name="norm_proj",
    )(x, g, w, cw, cb)


def _mlp_kernel(x_ref, g_ref, w1_ref, w2_ref, o_ref, h_scr, acc_scr):
    f = pl.program_id(1)

    @pl.when(f == 0)
    def _():
        h_scr[...] = _rms(x_ref[...], g_ref[...]).astype(BF16)
        acc_scr[...] = jnp.zeros_like(acc_scr)

    a = _dot(h_scr[...], w1_ref[...])
    a = jnp.square(jnp.maximum(a, 0.0)).astype(BF16)
    acc_scr[...] += _dot(a, w2_ref[...])

    @pl.when(f == pl.num_programs(1) - 1)
    def _():
        o_ref[...] = x_ref[...] + acc_scr[...]


def _mlp(x, g, w1, w2, layer):
    n, d = x.shape
    dff = w1.shape[2]
    tm = min(1024, n)
    tf = 1024
    return pl.pallas_call(
        _mlp_kernel,
        grid=(n // tm, dff // tf),
        in_specs=[
            pl.BlockSpec((tm, d), lambda i, f: (i, 0)),
            pl.BlockSpec((1, d), lambda i, f: (0, 0)),
            pl.BlockSpec((None, d, tf), lambda i, f: (layer, 0, f)),
            pl.BlockSpec((None, tf, d), lambda i, f: (layer, f, 0)),
        ],
        out_specs=pl.BlockSpec((tm, d), lambda i, f: (i, 0)),
        out_shape=jax.ShapeDtypeStruct((n, d), F32),
        scratch_shapes=[pltpu.VMEM((tm, d), BF16), pltpu.VMEM((tm, d), F32)],
        compiler_params=_params(("parallel", "arbitrary")),
        name="mlp",
    )(x, g, w1, w2)


def _outproj2_kernel(x_ref, ya_ref, yb_ref, wa_ref, wb_ref, o_ref):
    o_ref[...] = x_ref[...] + _dot(ya_ref[...], wa_ref[...]) + _dot(yb_ref[...], wb_ref[...])


def _outproj2(x, ya, yb, w, layer):
    n, d = x.shape
    ka, kb = ya.shape[1], yb.shape[1]
    assert ka % kb == 0
    tm = min(1024, n)
    row = lambda width: pl.BlockSpec((tm, width), lambda i: (i, 0))
    return pl.pallas_call(
        _outproj2_kernel,
        grid=(n // tm,),
        in_specs=[row(d), row(ka), row(kb),
                  pl.BlockSpec((None, ka, d), lambda i: (layer, 0, 0)),
                  pl.BlockSpec((None, kb, d), lambda i: (layer, ka // kb, 0))],
        out_specs=row(d),
        out_shape=jax.ShapeDtypeStruct((n, d), F32),
        compiler_params=_params(("parallel",)),
        name="outproj",
    )(x, ya, yb, w, w)


def _rope_kernel(inv_ref, cos_ref, sin_ref):
    rows = cos_ref.shape[0]
    pos = (pl.program_id(0) * rows + lax.broadcasted_iota(jnp.int32, (rows, RET_DK), 0)).astype(F32)
    lane = lax.broadcasted_iota(jnp.int32, (rows, RET_DK), 1)
    ang = pos * inv_ref[...]
    cos_ref[...] = jnp.cos(ang)
    sin_ref[...] = jnp.where(lane < RET_DK // 2, -jnp.sin(ang), jnp.sin(ang))


def _rope_tables(t):
    half = RET_DK // 2
    inv = ROPE_BASE ** (-jnp.arange(half, dtype=F32) / half)
    inv = jnp.concatenate([inv, inv])[None, :]
    rows = min(512, t)
    return pl.pallas_call(
        _rope_kernel,
        grid=(t // rows,),
        in_specs=[pl.BlockSpec((1, RET_DK), lambda i: (0, 0))],
        out_specs=[pl.BlockSpec((rows, RET_DK), lambda i: (i, 0))] * 2,
        out_shape=[jax.ShapeDtypeStruct((t, RET_DK), F32)] * 2,
        compiler_params=_params(("parallel",)),
        name="rope_tables",
    )(inv)


def _even_core_kernel(p_ref, px_ref, dt_ref, x_ref, cos_ref, sin_ref, qn_ref, kn_ref, gn_ref, dtb_ref, alog_ref,
                      dsk_ref, sg_ref, e_ref, tril_ref, wout_ref, o_ref, rstate, sstate, decay_scr, y_scr):
    t = pl.program_id(1)
    c = CHUNK
    row = lax.broadcasted_iota(jnp.int32, (c, c), 0)
    col = lax.broadcasted_iota(jnp.int32, (c, c), 1)
    causal = row >= col

    @pl.when(t == 0)
    def _init():
        rstate[...] = jnp.zeros_like(rstate)
        sstate[...] = jnp.zeros_like(sstate)
        y_scr[...] = jnp.zeros_like(y_scr)
        rel = (row - col).astype(F32)
        for h in range(RET_HEADS):
            decay_scr[h] = jnp.where(causal, jnp.exp(RET_LOG_GAMMA[h] * jnp.maximum(rel, 0.0)), 0.0)

    slot = t % 2
    o_ref[...] = x_ref[...] + _dot(y_scr[1 - slot], wout_ref[...])
    y_ref = y_scr.at[slot]
    idx = lax.broadcasted_iota(jnp.int32, (c, 1), 0).astype(F32)
    for ci in range(p_ref.shape[0] // c):
        _even_chunk(slice(ci * c, (ci + 1) * c), idx, causal, p_ref, px_ref, dt_ref, cos_ref, sin_ref, qn_ref, kn_ref,
                    gn_ref, dtb_ref, alog_ref, dsk_ref, sg_ref, e_ref, tril_ref, y_ref, rstate, sstate, decay_scr)


def _even_chunk(rows, idx, causal, p_ref, px_ref, dt_ref, cos_ref, sin_ref, qn_ref, kn_ref, gn_ref, dtb_ref, alog_ref,
                dsk_ref, sg_ref, e_ref, tril_ref, y_ref, rstate, sstate, decay_scr):
    c = CHUNK
    cos = cos_ref[rows, :]
    sin = sin_ref[rows, :]
    for h in range(RET_HEADS):
        lg = RET_LOG_GAMMA[h]
        qh = _rms(p_ref[rows, EV_Q + h * RET_DK:EV_Q + (h + 1) * RET_DK].astype(F32), qn_ref[...])
        kh = _rms(p_ref[rows, EV_K + h * RET_DK:EV_K + (h + 1) * RET_DK].astype(F32), kn_ref[...])
        qh = qh * cos + pltpu.roll(qh, RET_DK // 2, 1) * sin
        kh = (kh * cos + pltpu.roll(kh, RET_DK // 2, 1) * sin) * (RET_DK ** -0.5)
        vb = p_ref[rows, EV_V + h * RET_DV:EV_V + (h + 1) * RET_DV]
        scores = _dot_nt(qh.astype(BF16), kh.astype(BF16)) * decay_scr[h]
        inner = _dot(scores.astype(BF16), vb)
        s_prev = rstate[h]
        q_dec = jnp.exp(lg * (idx + 1.0))
        cross = _dot((qh * q_dec).astype(BF16), s_prev.astype(BF16))
        k_dec = jnp.exp(lg * (float(c - 1) - idx))
        rstate[h] = s_prev * math.exp(lg * c) + _dot_tn((kh * k_dec).astype(BF16), vb)
        ya = inner + cross
        yc = ya - jnp.mean(ya, axis=-1, keepdims=True)
        yn = yc * lax.rsqrt(jnp.mean(yc * yc, axis=-1, keepdims=True) + EPS)
        yn = yn * gn_ref[:, h * RET_DV:(h + 1) * RET_DV]
        gate = p_ref[rows, EV_G + h * RET_DV:EV_G + (h + 1) * RET_DV].astype(F32)
        y_ref[rows, h * RET_DV:(h + 1) * RET_DV] = (gate * yn).astype(y_ref.dtype)

    xs = px_ref[rows, :SSD_D_INNER].astype(F32)
    gs = SSD_GROUPS * SSD_STATE
    bm = px_ref[rows, SSD_D_INNER:SSD_D_INNER + gs]
    cm = px_ref[rows, SSD_D_INNER + gs:SSD_D_INNER + 2 * gs]

    dt = _softplus(dt_ref[rows, :] + dtb_ref[...])
    a_neg = -jnp.exp(alog_ref[...])
    d_a = dt * a_neg
    acum = _dot_exact_rhs(tril_ref[...], d_a)
    acum_t = acum.T
    exp_a = jnp.exp(acum)
    dec = jnp.exp(acum[c - 1:c, :] - acum)
    e_mat = e_ref[...]
    dt_e = _dot_exact_lhs(dt, e_mat)
    exp_a_e = _dot_exact_lhs(exp_a, e_mat)
    dec_e = _dot_exact_lhs(dec, e_mat)
    xr = xs * dt_e
    xrd_b = (xr * dec_e).astype(BF16)
    cdec_e = exp_a_e[c - 1:c, :]
    lane = lax.broadcasted_iota(jnp.int32, (c, 128), 1)
    left = lane < SSD_HEADDIM
    hpg = SSD_HEADS // SSD_GROUPS
    gw = hpg * SSD_HEADDIM
    z_gate = p_ref[rows, EV_Z:EV_Z + SSD_D_INNER].astype(F32)
    for g in range(SSD_GROUPS):
        bg = bm[:, g * SSD_STATE:(g + 1) * SSD_STATE]
        cg = cm[:, g * SSD_STATE:(g + 1) * SSD_STATE]
        cb_mat = _dot_nt(cg, bg)
        s_prev = sstate[g]
        y_off = _dot(cg, s_prev.astype(BF16)) * exp_a_e[:, g * gw:(g + 1) * gw]
        pairs = []
        for pr in range(hpg // 2):
            h0 = g * hpg + 2 * pr
            ms = []
            for h in (h0, h0 + 1):
                seg = acum[:, h:h + 1] - acum_t[h:h + 1, :]
                l_mat = jnp.where(causal, jnp.exp(jnp.minimum(seg, 0.0)), 0.0)
                ms.append((cb_mat * l_mat).astype(BF16))
            xp = xr[:, h0 * SSD_HEADDIM:(h0 + 2) * SSD_HEADDIM]
            xbd = jnp.concatenate([jnp.where(left, xp, 0.0), jnp.where(left, 0.0, xp)], axis=0).astype(BF16)
            pairs.append(_dot(jnp.concatenate(ms, axis=1), xbd))
        y_diag = jnp.concatenate(pairs, axis=1)
        sstate[g] = s_prev * cdec_e[:, g * gw:(g + 1) * gw] + _dot_tn(bg, xrd_b[:, g * gw:(g + 1) * gw])
        yb = y_diag + y_off + xs[:, g * gw:(g + 1) * gw] * dsk_ref[:, g * gw:(g + 1) * gw]
        yb = yb * z_gate[:, g * gw:(g + 1) * gw]
        yb = _rms(yb, sg_ref[:, g * gw:(g + 1) * gw])
        lo = RET_HEADS * RET_DV + g * gw
        y_ref[rows, lo:lo + gw] = yb.astype(y_ref.dtype)


def _even_core(p, px, dt, x, cos, sin, qn, kn, gn, dtb, alog, dsk, sg, e_mat, tril, w_out):
    b, t, _ = p.shape
    c = min(EVEN_ROWS, t)
    last = t // c - 1
    cur = lambda bi, ti: (bi, jnp.minimum(ti, last), 0)
    prev = lambda bi, ti: (bi, jnp.maximum(ti - 1, 0), 0)
    full = lambda a: pl.BlockSpec(a.shape, lambda bi, ti: (0,) * a.ndim)
    smalls = (qn, kn, gn, dtb, alog, dsk, sg, e_mat, tril, w_out)
    return pl.pallas_call(
        _even_core_kernel,
        grid=(b, t // c + 1),
        in_specs=[
            pl.BlockSpec((None, c, EV_XBC), cur),
            pl.BlockSpec((None, c, SSD_CONV_DIM), cur),
            pl.BlockSpec((None, c, EVEN_IN_PAD - EV_DT), cur),
            pl.BlockSpec((None, c, D_MODEL), prev),
            pl.BlockSpec((c, RET_DK), lambda bi, ti: (jnp.minimum(ti, last), 0)),
            pl.BlockSpec((c, RET_DK), lambda bi, ti: (jnp.minimum(ti, last), 0)),
        ] + [full(a) for a in smalls],
        out_specs=pl.BlockSpec((None, c, D_MODEL), prev),
        out_shape=jax.ShapeDtypeStruct((b, t, D_MODEL), F32),
        scratch_shapes=[
            pltpu.VMEM((RET_HEADS, RET_DK, RET_DV), F32),
            pltpu.VMEM((SSD_GROUPS, SSD_STATE, SSD_D_INNER // SSD_GROUPS), F32),
            pltpu.VMEM((RET_HEADS, CHUNK, CHUNK), F32),
            pltpu.VMEM((2, c, EVEN_OUT), BF16),
        ],
        compiler_params=_params(("parallel", "arbitrary")),
        name="even_core",
    )(p, px, dt, x, cos, sin, *smalls)


def _odd_core_kernel(pq_ref, pc_ref, wg_ref, ba_ref, bx_ref, lam_ref, qn_ref, kn_ref, mavg_ref,
                     yc_ref, q_ref, k_ref, v_ref, a_scr, b_scr, hcarry):
    t = pl.program_id(1)
    r = pc_ref.shape[0]

    @pl.when(t == 0)
    def _init():
        hcarry[...] = jnp.zeros_like(hcarry)

    xc = pc_ref[...].astype(F32)

    rs, is_ = [], []
    for blk in range(LRU_BLOCKS):
        g = _dot(xc[:, blk * LRU_BLOCK:(blk + 1) * LRU_BLOCK].astype(BF16), wg_ref[blk])
        rs.append(g[:, :LRU_BLOCK])
        is_.append(g[:, LRU_BLOCK:])
    rg = _sigmoid(jnp.concatenate(rs, axis=1) + ba_ref[...])
    ig = _sigmoid(jnp.concatenate(is_, axis=1) + bx_ref[...])
    log_a = -LRU_C * rg * _softplus(-lam_ref[...])
    a_scr[...] = jnp.exp(log_a)
    b_scr[...] = jnp.sqrt(1.0 - jnp.exp(2.0 * log_a)) * (ig * xc)

    rowi = lax.broadcasted_iota(jnp.int32, (SUBLANES, LRU_WIDTH), 0)

    def body(gi, carry):
        off = pl.multiple_of(gi * SUBLANES, SUBLANES)
        a = a_scr[pl.ds(off, SUBLANES), :]
        bv = b_scr[pl.ds(off, SUBLANES), :]
        for s in (1, 2, 4):
            a_sh = jnp.where(rowi >= s, pltpu.roll(a, s, 0), 1.0)
            b_sh = jnp.where(rowi >= s, pltpu.roll(bv, s, 0), 0.0)
            bv = a * b_sh + bv
            a = a * a_sh
        h = a * carry + bv
        b_scr[pl.ds(off, SUBLANES), :] = h
        return jnp.broadcast_to(h[SUBLANES - 1:SUBLANES, :], (SUBLANES, LRU_WIDTH))

    hcarry[...] = lax.fori_loop(0, r // SUBLANES, body, hcarry[...])
    yc_ref[...] = (b_scr[...] * pq_ref[:, 3 * SB_WIDTH:].astype(F32)).astype(yc_ref.dtype)

    def headnorm(x, g):
        sq = x * x
        hi = sq.astype(BF16)
        lo = (sq - hi.astype(F32)).astype(BF16)
        ms = _dot(hi, mavg_ref[...]) + _dot(lo, mavg_ref[...])
        return x * lax.rsqrt(ms + EPS) * g

    q = headnorm(pq_ref[:, 0:SB_WIDTH].astype(F32), qn_ref[...]) * (SB_HEAD_DIM ** -0.5 * LOG2E)
    q_ref[...] = q.astype(q_ref.dtype)
    kn = headnorm(pq_ref[:, SB_WIDTH:2 * SB_WIDTH].astype(F32), kn_ref[...])
    vt = pq_ref[:, 2 * SB_WIDTH:3 * SB_WIDTH].astype(F32).T
    pw = 2 * SB_HEAD_DIM
    lane = lax.broadcasted_iota(jnp.int32, (r, pw), 1)
    sub = lax.broadcasted_iota(jnp.int32, (pw, r), 0)
    for pr in range(SB_HEADS // 2):
        kp = kn[:, pr * pw:(pr + 1) * pw]
        k_ref[:, 2 * pr * pw:(2 * pr + 1) * pw] = jnp.where(lane < SB_HEAD_DIM, kp, 0.0).astype(k_ref.dtype)
        k_ref[:, (2 * pr + 1) * pw:(2 * pr + 2) * pw] = jnp.where(lane < SB_HEAD_DIM, 0.0, kp).astype(k_ref.dtype)
        vp = vt[pr * pw:(pr + 1) * pw, :]
        v_ref[2 * pr * pw:(2 * pr + 1) * pw, :] = jnp.where(sub < SB_HEAD_DIM, vp, 0.0).astype(v_ref.dtype)
        v_ref[(2 * pr + 1) * pw:(2 * pr + 2) * pw, :] = jnp.where(sub < SB_HEAD_DIM, 0.0, vp).astype(v_ref.dtype)


def _odd_core(pq, pc, wg, ba, bx, lam, qn, kn, mavg):
    b, t, _ = pq.shape
    r = min(512, t)
    full = lambda a: pl.BlockSpec(a.shape, lambda bi, ti: (0,) * a.ndim)
    smalls = (wg, ba, bx, lam, qn, kn, mavg)
    blk = lambda w: pl.BlockSpec((None, r, w), lambda bi, ti: (bi, ti, 0))
    return pl.pallas_call(
        _odd_core_kernel,
        grid=(b, t // r),
        in_specs=[blk(3 * SB_WIDTH + LRU_WIDTH), blk(LRU_WIDTH)] + [full(a) for a in smalls],
        out_specs=[blk(LRU_WIDTH), blk(SB_WIDTH), blk(2 * SB_WIDTH),
                   pl.BlockSpec((None, 2 * SB_WIDTH, r), lambda bi, ti: (bi, 0, ti))],
        out_shape=[jax.ShapeDtypeStruct((b, t, LRU_WIDTH), BF16), jax.ShapeDtypeStruct((b, t, SB_WIDTH), BF16),
                   jax.ShapeDtypeStruct((b, t, 2 * SB_WIDTH), BF16), jax.ShapeDtypeStruct((b, 2 * SB_WIDTH, t), BF16)],
        scratch_shapes=[
            pltpu.VMEM((r, LRU_WIDTH), F32),
            pltpu.VMEM((r, LRU_WIDTH), F32),
            pltpu.VMEM((SUBLANES, LRU_WIDTH), F32),
        ],
        compiler_params=_params(("parallel", "arbitrary")),
        name="odd_core",
    )(pq, pc, *smalls)


SB_TQ = 512
SB_TK = 128
SB_UNROLL = 2
SB_STAGES = 5


def _sb_attn_kernel(q_ref, k_ref, vt_ref, ut_ref, o_ref, z_buf, lb_buf, l1b_buf, r0_buf, r0_prev, c_buf, sfx_buf, w_buf, acc):
    qi = pl.program_id(2)
    tq = q_ref.shape[0]
    tk = SB_TK
    pw = 2 * SB_HEAD_DIM
    ndiag = tq // tk
    jmax = qi * ndiag + (ndiag - 1)
    kloc = lax.broadcasted_iota(jnp.int32, (tk, tq), 0)
    qpos = qi * tq + lax.broadcasted_iota(jnp.int32, (tk, tq), 1)

    def key_start(t):
        return pl.multiple_of((jmax - t) * tk, tk)

    def p0(t):
        kx = k_ref[pl.ds(key_start(t), tk), :]
        km = jnp.concatenate([kx[:, :pw], kx[:, pw:]], axis=0)
        z_buf[...] = _dot_nt(km, q_ref[...])

    def p1(t, slot, masked):
        if masked:
            valid = (key_start(t) + kloc) < qpos
        for h in range(2):
            z = z_buf[h * tk:(h + 1) * tk, :]
            sp = jnp.log(1.0 + jnp.exp2(-jnp.abs(z))) * LOG2E
            lb = jnp.minimum(z, 0.0) - sp
            l1 = lb - z
            if masked:
                l1 = jnp.where(valid, l1, 0.0)
            lb_buf[slot, h * tk:(h + 1) * tk, :] = lb
            l1b_buf[:, h * tq:(h + 1) * tq] = l1.astype(BF16)
            r0_buf[:, h * tq:(h + 1) * tq] = l1[0:1, :]

    def p2():
        sfx_buf[...] = _dot(ut_ref[...], l1b_buf[...])
        r0_prev[...] = r0_buf[...]

    def p3(t, slot, masked):
        if masked:
            valid = (key_start(t) + kloc) < qpos
        carry = c_buf[...]
        for h in range(2):
            sfx = sfx_buf[:, h * tq:(h + 1) * tq] + carry[:, h * tq:(h + 1) * tq]
            w = jnp.exp2(lb_buf[slot, h * tk:(h + 1) * tk, :] + sfx)
            if masked:
                w = jnp.where(valid, w, 0.0)
            w_buf[h * tk:(h + 1) * tk, :] = w.astype(BF16)
        c_buf[...] = carry + sfx_buf[0:1, :] + r0_prev[...]

    def p4(t):
        vx = vt_ref[:, pl.ds(key_start(t), tk)]
        vbd = jnp.concatenate([vx[:pw], vx[pw:]], axis=1)
        acc[...] += _dot(vbd, w_buf[...])

    def emit(i, active, parity, masked):
        if 4 in active:
            p4(i - 4)
        if 3 in active:
            p3(i - 3, (parity - 3) % 2, masked(i - 3))
        if 2 in active:
            p2()
        if 1 in active:
            p1(i - 1, (parity - 1) % 2, masked(i - 1))
        if 0 in active:
            p0(i)

    def static_iterations(n_tiles, iters):
        for i in iters:
            active = {k for k in range(SB_STAGES) if 0 <= i - k < n_tiles}
            emit(i, active, i % 2, lambda t: t < ndiag)

    acc[...] = jnp.zeros_like(acc)
    c_buf[...] = jnp.zeros_like(c_buf)

    @pl.when(qi == 0)
    def _diagonal_only():
        static_iterations(ndiag, range(ndiag + SB_STAGES - 1))

    @pl.when(qi > 0)
    def _full():
        n = (qi + 1) * ndiag
        head = 2 * ndiag
        static_iterations(head, range(head))
        never = lambda t: False

        def body(jj, _):
            for u in range(SB_UNROLL):
                emit(head + SB_UNROLL * jj + u, set(range(SB_STAGES)), u % 2, never)
            return 0

        lax.fori_loop(0, (n - head) // SB_UNROLL, body, 0)
        for e in range(SB_STAGES - 1):
            emit(n + e, {k for k in range(SB_STAGES) if k > e}, e % 2, never)

    o_ref[...] = acc[...].T.astype(o_ref.dtype)


def _sb_attention(q, kx, vtx, ut):
    b, t, _ = q.shape
    tq = min(SB_TQ, t)
    tk = SB_TK
    pairs = SB_HEADS // 2
    pw = 2 * SB_HEAD_DIM
    return pl.pallas_call(
        _sb_attn_kernel,
        grid=(b, pairs, t // tq),
        in_specs=[
            pl.BlockSpec((None, tq, pw), lambda bi, pi, qi: (bi, qi, pi)),
            pl.BlockSpec((None, t, 2 * pw), lambda bi, pi, qi: (bi, 0, pi)),
            pl.BlockSpec((None, 2 * pw, t), lambda bi, pi, qi: (bi, pi, 0)),
            pl.BlockSpec(ut.shape, lambda bi, pi, qi: (0, 0)),
        ],
        out_specs=pl.BlockSpec((None, tq, pw), lambda bi, pi, qi: (bi, qi, pi)),
        out_shape=jax.ShapeDtypeStruct((b, t, SB_WIDTH), BF16),
        scratch_shapes=[
            pltpu.VMEM((2 * tk, tq), F32),
            pltpu.VMEM((2, 2 * tk, tq), F32),
            pltpu.VMEM((tk, 2 * tq), BF16),
            pltpu.VMEM((1, 2 * tq), F32),
            pltpu.VMEM((1, 2 * tq), F32),
            pltpu.VMEM((1, 2 * tq), F32),
            pltpu.VMEM((tk, 2 * tq), F32),
            pltpu.VMEM((2 * tk, tq), BF16),
            pltpu.VMEM((pw, tq), F32),
        ],
        compiler_params=_params(("parallel", "parallel", "arbitrary")),
        name="sb_attention",
    )(q, kx, vtx, ut)


def _const_tables():
    i = np.arange(128)
    tril = (i[None, :] <= i[:, None]).astype(np.float32)
    e_mat = np.zeros((128, SSD_D_INNER), np.float32)
    for h in range(SSD_HEADS):
        e_mat[h, h * SSD_HEADDIM:(h + 1) * SSD_HEADDIM] = 1.0
    uo = (i[None, :] > i[:, None]).astype(np.float32)
    hd = np.arange(SB_WIDTH) // SB_HEAD_DIM
    mavg = (hd[:, None] == hd[None, :]).astype(np.float32) / SB_HEAD_DIM
    return (jnp.asarray(tril, BF16), jnp.asarray(e_mat, BF16), jnp.asarray(uo, BF16), jnp.asarray(mavg, BF16))


def _row(v):
    return v.reshape(1, -1).astype(F32)


def _pad_lanes(v, width=128):
    return jnp.pad(v.reshape(1, -1).astype(F32), ((0, 0), (0, width - v.shape[-1])))


def kernel(x, norm_mix, norm_mlp, mlp_w1, mlp_w2, ev_w_in, ev_w_out, ret_qn, ret_kn, ret_gn, ssd_conv_w, ssd_conv_b, ssd_dt_bias, ssd_a_log, ssd_d, ssd_norm, od_w_in, od_w_out, lru_conv_w, lru_conv_b, lru_wa, lru_ba, lru_wx, lru_bx, lru_lam, sb_qn, sb_kn):
    b, t, d = x.shape
    n = b * t
    depth = norm_mix.shape[0]
    tril, e_mat, uo, mavg = _const_tables()
    cos, sin = _rope_tables(t)
    xf = x.reshape(n, d)
    mlp_w1b, mlp_w2b = mlp_w1.astype(BF16), mlp_w2.astype(BF16)
    ev_w_in_b = jnp.pad(ev_w_in, ((0, 0), (0, 0), (0, EVEN_IN_PAD - EVEN_IN))).astype(BF16)
    od_w_in_b, od_w_out_b = od_w_in.astype(BF16), od_w_out.astype(BF16)
    for l in range(depth):
        g_mix = _row(norm_mix[l])
        if l % 2 == 0:
            e = l // 2
            plan = ((EV_XBC, EV_DT, "conv_silu", 1, 0), (EV_G, EV_XBC, "silu", 0, EV_G),
                    (EV_DT, EVEN_IN_PAD, "none", 2, 0), (EV_Q, EV_G, "none", 0, 0))
            p, px, dt = _norm_proj(xf, g_mix, ev_w_in_b, e, ssd_conv_w[e].astype(F32), _row(ssd_conv_b[e]), plan,
                                   (EV_XBC, SSD_CONV_DIM, EVEN_IN_PAD - EV_DT), (BF16, BF16, F32), t)
            xf = _even_core(
                p.reshape(b, t, EV_XBC), px.reshape(b, t, SSD_CONV_DIM), dt.reshape(b, t, EVEN_IN_PAD - EV_DT),
                xf.reshape(b, t, d), cos, sin,
                _row(ret_qn[e]), _row(ret_kn[e]), _row(ret_gn[e]), _pad_lanes(ssd_dt_bias[e]), _pad_lanes(ssd_a_log[e]),
                _row(jnp.repeat(ssd_d[e], SSD_HEADDIM)), _row(ssd_norm[e]), e_mat, tril,
                ev_w_out[e].astype(BF16)).reshape(n, d)
        else:
            o = l // 2
            plan = ((OD_XC, OD_Q, "conv", 1, 0), (OD_GATE, OD_XC, "gelu", 0, ODD_IN - OD_Q), (OD_Q, ODD_IN, "none", 0, 0))
            pq, pc = _norm_proj(xf, g_mix, od_w_in_b, o, lru_conv_w[o].astype(F32), _row(lru_conv_b[o]), plan,
                                (ODD_IN - OD_Q + LRU_WIDTH, LRU_WIDTH), (BF16, BF16), t)
            wg = jnp.concatenate([lru_wa[o], lru_wx[o]], axis=-1).astype(BF16)
            yc, q, k, v = _odd_core(
                pq.reshape(b, t, 3 * SB_WIDTH + LRU_WIDTH), pc.reshape(b, t, LRU_WIDTH),
                wg, _row(lru_ba[o]), _row(lru_bx[o]),
                _row(lru_lam[o]), _row(jnp.tile(sb_qn[o], SB_HEADS)), _row(jnp.tile(sb_kn[o], SB_HEADS)), mavg)
            yd = _sb_attention(q, k, v, uo)
            xf = _outproj2(xf, yc.reshape(n, LRU_WIDTH), yd.reshape(n, SB_WIDTH), od_w_out_b, o)
        xf = _mlp(xf, _row(norm_mlp[l]), mlp_w1b, mlp_w2b, l)
    return xf.reshape(b, t, d)
```
